```python
import math
import jax, jax.numpy as jnp
from jax import lax
import numpy as np

D_MODEL = 1024
BATCH = 8
SEQ = 8192
DEPTH = 1
DEC_BATCH = 1
DEC_SEQ = 16384
PAST_LEN = 128

N_MEM = 256
D_FF = 2816
CONV_WIDTH = 512
CONV_K = 31
SSM_WIDTH = 512
SSM_GROUP = 16
SSM_GROUPS = SSM_WIDTH // SSM_GROUP
SSM_STATE = 64
N_BRANCH = 2
IN_COLS = 2 * CONV_WIDTH + SSM_WIDTH + N_BRANCH * D_MODEL
X_HEADS = 4
X_HEAD_DIM = D_MODEL // X_HEADS
EPS = 1e-6
DT_MIN = 1e-3
DT_MAX = 1e-1

kernel_name = 'conformer_s5_gated_encoder'


def rmsnorm(x, g):
    xf = x.astype(jnp.float32)
    y = xf * lax.rsqrt(jnp.mean(xf * xf, axis=-1, keepdims=True) + EPS)
    return (y * g.astype(jnp.float32)).astype(x.dtype)


def layernorm(x, g, b):
    xf = x.astype(jnp.float32)
    xc = xf - jnp.mean(xf, axis=-1, keepdims=True)
    y = xc * lax.rsqrt(jnp.mean(xc * xc, axis=-1, keepdims=True) + EPS)
    return (y * g.astype(jnp.float32) + b.astype(jnp.float32)).astype(x.dtype)


def swiglu_ffn(x, w_gu, w_down):
    gate, up = jnp.split(x @ w_gu, 2, axis=-1)
    return (jax.nn.silu(gate) * up) @ w_down


def conv_module(z, w_dw, b_dw, ln_g, ln_b, w_pw):
    v, gate = jnp.split(z, 2, axis=-1)
    v = v * jax.nn.sigmoid(gate)
    pad = CONV_K // 2
    v = lax.conv_general_dilated(
        v, w_dw[:, None, :].astype(v.dtype), window_strides=(1,),
        padding=[(pad, pad)], dimension_numbers=('NWC', 'WIO', 'NWC'),
        feature_group_count=CONV_WIDTH) + b_dw
    v = jax.nn.silu(layernorm(v, ln_g, ln_b))
    return v @ w_pw


def _complex(re, im):
    return lax.complex(re.astype(jnp.float32), im.astype(jnp.float32))


def zoh_discretize(lam_re, lam_im, log_dt, b_re, b_im):
    lam = _complex(lam_re, lam_im)
    dt = jnp.exp(log_dt.astype(jnp.float32))[:, None]
    lam_bar = jnp.exp(lam * dt)
    b_bar = ((lam_bar - 1.0) / lam)[..., None] * _complex(b_re, b_im)
    return lam_bar, b_bar


def _linear_recurrence(e1, e2):
    a1, b1 = e1
    a2, b2 = e2
    return a1 * a2, a2 * b1 + b2


def diag_scan(u, lam_bar, b_bar, c, reverse):
    bu = jnp.einsum('lgh,gnh->lgn', u, b_bar)
    a = jnp.broadcast_to(lam_bar, bu.shape)
    _, s = lax.associative_scan(_linear_recurrence, (a, bu), reverse=reverse, axis=0)
    return jnp.einsum('lgn,ghn->lgh', s, c).real


def s5_branch(u, lam_re, lam_im, log_dt, b_re, b_im, c_re, c_im, d_skip, w_glu):
    lam_f, b_f = zoh_discretize(lam_re[0], lam_im[0], log_dt[0], b_re[0], b_im[0])
    lam_b, b_b = zoh_discretize(lam_re[1], lam_im[1], log_dt[1], b_re[1], b_im[1])
    c_f = _complex(c_re[0], c_im[0])
    c_b = _complex(c_re[1], c_im[1])
    d = d_skip.astype(jnp.float32).reshape(SSM_GROUPS, SSM_GROUP)

    def one_sequence(us):
        length = us.shape[0]
        ug = us.astype(jnp.float32).reshape(length, SSM_GROUPS, SSM_GROUP)
        uc = ug.astype(jnp.complex64)
        y = (diag_scan(uc, lam_f, b_f, c_f, False)
             + diag_scan(uc, lam_b, b_b, c_b, True)
             + d * ug)
        return y.reshape(length, SSM_WIDTH)

    y = lax.map(one_sequence, u).astype(u.dtype)
    a, g = jnp.split(jax.nn.gelu(y) @ w_glu, 2, axis=-1)
    return a * jax.nn.sigmoid(g)


def memory_cross_attention(q_in, mem, mem_g, w_q, w_kv, w_o):
    bsz, length, _ = q_in.shape
    m = rmsnorm(mem, mem_g)
    q = (q_in @ w_q).reshape(bsz, length, X_HEADS, X_HEAD_DIM)
    k, v = jnp.split(m @ w_kv, 2, axis=-1)
    k = k.reshape(bsz, N_MEM, X_HEADS, X_HEAD_DIM)
    v = v.reshape(bsz, N_MEM, X_HEADS, X_HEAD_DIM)
    s = jnp.einsum('bqhd,bkhd->bhqk', q, k).astype(jnp.float32) * (X_HEAD_DIM ** -0.5)
    p = jax.nn.softmax(s, axis=-1).astype(v.dtype)
    o = jnp.einsum('bhqk,bkhd->bqhd', p, v).reshape(bsz, length, D_MODEL)
    return o @ w_o


def encoder_layer(h, mem, p):
    h = h + 0.5 * swiglu_ffn(rmsnorm(h, p['ffn1_g']), p['ffn1_wgu'], p['ffn1_wd'])
    u = rmsnorm(h, p['mix_g'])
    z = u @ p['w_in'] + p['b_in']
    z_conv = z[..., :2 * CONV_WIDTH]
    z_ssm = z[..., 2 * CONV_WIDTH:2 * CONV_WIDTH + SSM_WIDTH]
    z_gate = z[..., 2 * CONV_WIDTH + SSM_WIDTH:]
    conv_out = conv_module(z_conv, p['conv_w'], p['conv_b'], p['conv_ln_g'], p['conv_ln_b'],
                           p['conv_w_pw'])
    ssm_out = s5_branch(z_ssm, p['ssm_lam_re'], p['ssm_lam_im'], p['ssm_log_dt'],
                        p['ssm_b_re'], p['ssm_b_im'], p['ssm_c_re'], p['ssm_c_im'],
                        p['ssm_d'], p['ssm_w_glu'])
    g_conv, g_ssm = jnp.split(jax.nn.sigmoid(z_gate), 2, axis=-1)
    h = h + (g_conv * conv_out + g_ssm * ssm_out) @ p['w_out']
    h = h + memory_cross_attention(rmsnorm(h, p['xattn_g']), mem, p['mem_g'],
                                   p['xattn_wq'], p['xattn_wkv'], p['xattn_wo'])
    h = h + 0.5 * swiglu_ffn(rmsnorm(h, p['ffn2_g']), p['ffn2_wgu'], p['ffn2_wd'])
    return h


def run_trunk(x, mem, layer_params, final_g):
    h = x
    for l in range(DEPTH):
        p = {name: arr[l] for name, arr in layer_params.items()}
        h = encoder_layer(h, mem, p)
    return rmsnorm(h, final_g)


def setup_inputs(seed: int = 0) -> dict:
    key = jax.random.key(seed)
    keys = iter(jax.random.split(key, 48))
    f32 = jnp.float32

    def nrm(shape, scale):
        return scale * jax.random.normal(next(keys), shape, f32)

    def gain(shape):
        return 1.0 + 0.02 * jax.random.normal(next(keys), shape, f32)

    G, N, H = SSM_GROUPS, SSM_STATE, SSM_GROUP
    lam_re = -0.5 + 0.01 * jax.random.normal(next(keys), (DEPTH, 2, G, N), f32)
    lam_im = (math.pi * jnp.arange(N, dtype=f32))[None, None, None, :] \
        + 0.01 * jax.random.normal(next(keys), (DEPTH, 2, G, N), f32)
    log_dt = jax.random.uniform(next(keys), (DEPTH, 2, G), f32,
                                minval=math.log(DT_MIN), maxval=math.log(DT_MAX))
    return {
        'x_prompt': nrm((BATCH, SEQ, D_MODEL), 1.0),
        'x_sample': nrm((DEC_BATCH, DEC_SEQ, D_MODEL), 1.0),
        'mem_prompt': nrm((BATCH, N_MEM, D_MODEL), 1.0),
        'mem_sample': nrm((DEC_BATCH, N_MEM, D_MODEL), 1.0),
        'ffn1_g': gain((DEPTH, D_MODEL)),
        'ffn1_wgu': nrm((DEPTH, D_MODEL, 2 * D_FF), D_MODEL ** -0.5),
        'ffn1_wd': nrm((DEPTH, D_FF, D_MODEL), D_FF ** -0.5),
        'mix_g': gain((DEPTH, D_MODEL)),
        'w_in': nrm((DEPTH, D_MODEL, IN_COLS), D_MODEL ** -0.5),
        'b_in': nrm((DEPTH, IN_COLS), 0.02),
        'conv_w': nrm((DEPTH, CONV_K, CONV_WIDTH), CONV_K ** -0.5),
        'conv_b': nrm((DEPTH, CONV_WIDTH), 0.02),
        'conv_ln_g': gain((DEPTH, CONV_WIDTH)),
        'conv_ln_b': nrm((DEPTH, CONV_WIDTH), 0.02),
        'conv_w_pw': nrm((DEPTH, CONV_WIDTH, D_MODEL), CONV_WIDTH ** -0.5),
        'ssm_lam_re': lam_re,
        'ssm_lam_im': lam_im,
        'ssm_log_dt': log_dt,
        'ssm_b_re': nrm((DEPTH, 2, G, N, H), (2.0 * H) ** -0.5),
        'ssm_b_im': nrm((DEPTH, 2, G, N, H), (2.0 * H) ** -0.5),
        'ssm_c_re': nrm((DEPTH, 2, G, H, N), (2.0 * N) ** -0.5),
        'ssm_c_im': nrm((DEPTH, 2, G, H, N), (2.0 * N) ** -0.5),
        'ssm_d': nrm((DEPTH, SSM_WIDTH), 1.0),
        'ssm_w_glu': nrm((DEPTH, SSM_WIDTH, 2 * D_MODEL), SSM_WIDTH ** -0.5),
        'w_out': nrm((DEPTH, D_MODEL, D_MODEL), D_MODEL ** -0.5),
        'xattn_g': gain((DEPTH, D_MODEL)),
        'mem_g': gain((DEPTH, D_MODEL)),
        'xattn_wq': nrm((DEPTH, D_MODEL, D_MODEL), D_MODEL ** -0.5),
        'xattn_wkv': nrm((DEPTH, D_MODEL, 2 * D_MODEL), D_MODEL ** -0.5),
        'xattn_wo': nrm((DEPTH, D_MODEL, D_MODEL), D_MODEL ** -0.5),
        'ffn2_g': gain((DEPTH, D_MODEL)),
        'ffn2_wgu': nrm((DEPTH, D_MODEL, 2 * D_FF), D_MODEL ** -0.5),
        'ffn2_wd': nrm((DEPTH, D_FF, D_MODEL), D_FF ** -0.5),
        'final_g': gain((D_MODEL,)),
    }


def reference(x_prompt, x_sample, mem_prompt, mem_sample,
              ffn1_g, ffn1_wgu, ffn1_wd,
              mix_g, w_in, b_in,
              conv_w, conv_b, conv_ln_g, conv_ln_b, conv_w_pw,
              ssm_lam_re, ssm_lam_im, ssm_log_dt, ssm_b_re, ssm_b_im, ssm_c_re, ssm_c_im,
              ssm_d, ssm_w_glu,
              w_out,
              xattn_g, mem_g, xattn_wq, xattn_wkv, xattn_wo,
              ffn2_g, ffn2_wgu, ffn2_wd,
              final_g):
    layer_params = dict(
        ffn1_g=ffn1_g, ffn1_wgu=ffn1_wgu, ffn1_wd=ffn1_wd,
        mix_g=mix_g, w_in=w_in, b_in=b_in,
        conv_w=conv_w, conv_b=conv_b, conv_ln_g=conv_ln_g, conv_ln_b=conv_ln_b,
        conv_w_pw=conv_w_pw,
        ssm_lam_re=ssm_lam_re, ssm_lam_im=ssm_lam_im, ssm_log_dt=ssm_log_dt,
        ssm_b_re=ssm_b_re, ssm_b_im=ssm_b_im, ssm_c_re=ssm_c_re, ssm_c_im=ssm_c_im,
        ssm_d=ssm_d, ssm_w_glu=ssm_w_glu,
        w_out=w_out,
        xattn_g=xattn_g, mem_g=mem_g, xattn_wq=xattn_wq, xattn_wkv=xattn_wkv,
        xattn_wo=xattn_wo,
        ffn2_g=ffn2_g, ffn2_wgu=ffn2_wgu, ffn2_wd=ffn2_wd,
    )
    y_prompt = run_trunk(x_prompt, mem_prompt, layer_params, final_g)
    y_sample = run_trunk(x_sample, mem_sample, layer_params, final_g)
    return (y_prompt, y_sample)
```

```python
import functools
import math

import jax
import jax.numpy as jnp
from jax import lax
from jax.experimental import pallas as pl
from jax.experimental.pallas import tpu as pltpu

D_MODEL = 1024
D_FF = 2816
FF_CHUNK = 256
N_FF_CHUNKS = D_FF // FF_CHUNK
CONV_WIDTH = 512
CONV_K = 31
CONV_PAD = CONV_K // 2
HALO = 16
SSM_WIDTH = 512
SSM_GROUP = 16
SSM_GROUPS = SSM_WIDTH // SSM_GROUP
SSM_STATE = 64
CHUNK = 16
FLAT = CHUNK * SSM_GROUP
GROUPS_PER_BLOCK = 8
N_MEM = 256
X_HEADS = 4
X_HEAD_DIM = D_MODEL // X_HEADS
EPS = 1e-6
IN_COLS = 2 * CONV_WIDTH + SSM_WIDTH + 2 * D_MODEL

ROW_TILE = 512
FFN_ROW_TILE = 256
VMEM_LIMIT = 56 * 1024 * 1024

_F32 = jnp.float32
_BF16 = jnp.bfloat16


def _dot(a, b):
    return jnp.dot(a, b, preferred_element_type=_F32)


def _rmsnorm(x, g):
    return x * lax.rsqrt(jnp.mean(x * x, axis=-1, keepdims=True) + EPS) * g


def _const_spec(shape):
    nd = len(shape)
    return pl.BlockSpec(shape, lambda *_: (0,) * nd, pipeline_mode=pl.Buffered(1))


def _swiglu(xn, wg_ref, wu_ref, wd_ref, act_ref):
    for j in range(N_FF_CHUNKS):
        g = _dot(xn, wg_ref[j])
        u = _dot(xn, wu_ref[j])
        act_ref[:, j * FF_CHUNK:(j + 1) * FF_CHUNK] = (g * jax.nn.sigmoid(g) * u).astype(_BF16)
    return _dot(act_ref[...], wd_ref[...])


def _ffn_inproj_kernel(x_ref, g1_ref, wg_ref, wu_ref, wd_ref, gmix_ref, win_ref, bin_ref,
                       h1_ref, v_ref, zs_ref, gate_ref, act_ref):
    x = x_ref[...]
    xn = _rmsnorm(x, g1_ref[...]).astype(_BF16)
    h1 = x + 0.5 * _swiglu(xn, wg_ref, wu_ref, wd_ref, act_ref)
    h1_ref[...] = h1
    un = _rmsnorm(h1, gmix_ref[...]).astype(_BF16)
    z = _dot(un, win_ref[...]) + bin_ref[...]
    v_ref[...] = z[:, :CONV_WIDTH] * jax.nn.sigmoid(z[:, CONV_WIDTH:2 * CONV_WIDTH])
    zs_ref[...] = z[:, 2 * CONV_WIDTH:2 * CONV_WIDTH + SSM_WIDTH]
    gate_ref[...] = jax.nn.sigmoid(z[:, 2 * CONV_WIDTH + SSM_WIDTH:])


def _ffn_inproj(x, g1, wg, wu, wd, gmix, win, b_in):
    rows = x.shape[0]
    tm = FFN_ROW_TILE
    row = lambda w: pl.BlockSpec((tm, w), lambda i: (i, 0))
    return pl.pallas_call(
        _ffn_inproj_kernel,
        grid=(rows // tm,),
        in_specs=[row(D_MODEL), _const_spec((1, D_MODEL)),
                  _const_spec(wg.shape), _const_spec(wu.shape), _const_spec(wd.shape),
                  _const_spec((1, D_MODEL)), _const_spec(win.shape), _const_spec((1, IN_COLS))],
        out_specs=[row(D_MODEL), row(CONV_WIDTH), row(SSM_WIDTH), row(2 * D_MODEL)],
        out_shape=[jax.ShapeDtypeStruct((rows, D_MODEL), _F32),
                   jax.ShapeDtypeStruct((rows, CONV_WIDTH), _F32),
                   jax.ShapeDtypeStruct((rows, SSM_WIDTH), _F32),
                   jax.ShapeDtypeStruct((rows, 2 * D_MODEL), _F32)],
        scratch_shapes=[pltpu.VMEM((tm, D_FF), _BF16)],
        compiler_params=pltpu.CompilerParams(dimension_semantics=("arbitrary",),
                                             vmem_limit_bytes=VMEM_LIMIT),
        name="ffn_inproj",
    )(x, g1, wg, wu, wd, gmix, win, b_in)


def _ssm_scan_kernel(a_ref, r1_ref, r2_ref, coef_ref, y_ref, sf_ref, sfw_ref, sb_ref, sbw_ref,
                     *, n_chunks):
    nb = GROUPS_PER_BLOCK
    for g in range(nb):
        r = _dot(a_ref[g], r1_ref[g])
        y_ref[g] = r[:, :FLAT] + r[:, FLAT:2 * FLAT]
        base = 2 * FLAT
        for k, ref in enumerate((sf_ref, sfw_ref, sb_ref, sbw_ref)):
            ref[pl.ds(g, n_chunks, stride=nb), :] = r[:, base + 128 * k:base + 128 * (k + 1)]

    zero = jnp.zeros((nb, 128), _F32)

    def step(c, carry, s_ref, sw_ref, a, b, bp):
        v, w = carry
        rows = pl.ds(pl.multiple_of(c * nb, nb), nb)
        s = s_ref[rows, :]
        s_ref[rows, :] = v
        return a * v + b * w + s, a * w + bp * v + sw_ref[rows, :]

    fwd = functools.partial(step, s_ref=sf_ref, sw_ref=sfw_ref,
                            a=coef_ref[0], b=coef_ref[1], bp=coef_ref[2])
    lax.fori_loop(0, n_chunks, fwd, (zero, zero), unroll=8)
    bwd = functools.partial(step, s_ref=sb_ref, sw_ref=sbw_ref,
                            a=coef_ref[3], b=coef_ref[4], bp=coef_ref[5])
    lax.fori_loop(0, n_chunks, lambda i, carry: bwd(n_chunks - 1 - i, carry), (zero, zero),
                  unroll=8)

    for g in range(nb):
        xs = jnp.concatenate([sf_ref[pl.ds(g, n_chunks, stride=nb), :],
                              sb_ref[pl.ds(g, n_chunks, stride=nb), :]], axis=1)
        y_ref[g] += _dot(xs.astype(_BF16), r2_ref[g])


def _ssm_scan(a, r1, r2, coef):
    bsz, _, n_chunks, _ = a.shape
    nb = GROUPS_PER_BLOCK
    packed = pltpu.VMEM((n_chunks * nb, 128), _F32)
    once = pl.Buffered(1)
    return pl.pallas_call(
        functools.partial(_ssm_scan_kernel, n_chunks=n_chunks),
        grid=(SSM_GROUPS // nb, bsz),
        in_specs=[pl.BlockSpec((None, nb, n_chunks, FLAT), lambda j, b: (b, j, 0, 0)),
                  pl.BlockSpec((nb, FLAT, 4 * FLAT), lambda j, b: (j, 0, 0), pipeline_mode=once),
                  pl.BlockSpec((nb, FLAT, FLAT), lambda j, b: (j, 0, 0), pipeline_mode=once),
                  pl.BlockSpec((None, 6, nb, 128), lambda j, b: (j, 0, 0, 0), pipeline_mode=once)],
        out_specs=pl.BlockSpec((None, nb, n_chunks, FLAT), lambda j, b: (b, j, 0, 0)),
        out_shape=jax.ShapeDtypeStruct(a.shape, _F32),
        scratch_shapes=[packed] * 4,
        compiler_params=pltpu.CompilerParams(dimension_semantics=("arbitrary", "arbitrary"),
                                             vmem_limit_bytes=VMEM_LIMIT),
        name="ssm_scan",
    )(a, r1, r2, coef)


def _ssm_direction_tables(lam_re, lam_im, log_dt, b_re, b_im, c_re, c_im):
    dt = jnp.exp(log_dt)[:, None]
    mag = jnp.exp(lam_re * dt)
    lr, li = mag * jnp.cos(lam_im * dt), mag * jnp.sin(lam_im * dt)
    den = lam_re * lam_re + lam_im * lam_im
    qr = ((lr - 1.0) * lam_re + li * lam_im) / den
    qi = (li * lam_re - (lr - 1.0) * lam_im) / den
    bb_re = qr[..., None] * b_re - qi[..., None] * b_im
    bb_im = qr[..., None] * b_im + qi[..., None] * b_re
    p_re, p_im = [jnp.ones_like(lr)], [jnp.zeros_like(lr)]
    for _ in range(CHUNK):
        p_re, p_im = (p_re + [p_re[-1] * lr - p_im[-1] * li],
                      p_im + [p_re[-1] * li + p_im[-1] * lr])
    p_re, p_im = jnp.stack(p_re), jnp.stack(p_im)
    cp_re = c_re[None] * p_re[:, :, None, :] - c_im[None] * p_im[:, :, None, :]
    cp_im = c_re[None] * p_im[:, :, None, :] + c_im[None] * p_re[:, :, None, :]
    kern = (jnp.einsum('kgpn,gnh->kgph', cp_re[:CHUNK], bb_re, precision=lax.Precision.HIGHEST)
            - jnp.einsum('kgpn,gnh->kgph', cp_im[:CHUNK], bb_im, precision=lax.Precision.HIGHEST))
    return kern, p_re, p_im, bb_re, bb_im, cp_re, cp_im


def _ssm_tables(lam_re, lam_im, log_dt, b_re, b_im, c_re, c_im):
    G, H, N = SSM_GROUPS, SSM_GROUP, SSM_STATE
    lag = jnp.arange(CHUNK)[None, :] - jnp.arange(CHUNK)[:, None]
    r1_parts, r2_parts, coefs = [], [], []
    for d in range(2):
        kern, p_re, p_im, bb_re, bb_im, cp_re, cp_im = _ssm_direction_tables(
            lam_re[d], lam_im[d], log_dt[d], b_re[d], b_im[d], c_re[d], c_im[d])
        rel = lag if d == 0 else -lag
        toe = jnp.where((rel >= 0)[:, :, None, None, None],
                        kern[jnp.clip(rel, 0, CHUNK - 1)], 0.0)
        m = toe.transpose(2, 0, 4, 1, 3).reshape(G, FLAT, FLAT)
        pw = jnp.arange(CHUNK - 1, -1, -1) if d == 0 else jnp.arange(CHUNK)
        w_re = p_re[pw][:, :, :, None] * bb_re[None] - p_im[pw][:, :, :, None] * bb_im[None]
        w_im = p_re[pw][:, :, :, None] * bb_im[None] + p_im[pw][:, :, :, None] * bb_re[None]
        w_re = w_re.transpose(1, 0, 3, 2).reshape(G, FLAT, N)
        w_im = w_im.transpose(1, 0, 3, 2).reshape(G, FLAT, N)
        r1_parts.append((m, jnp.concatenate([w_re, w_im, w_im, w_re], axis=-1)))
        po = jnp.arange(1, CHUNK + 1) if d == 0 else jnp.arange(CHUNK, 0, -1)
        v_re = cp_re[po].transpose(1, 3, 0, 2).reshape(G, N, FLAT)
        v_im = cp_im[po].transpose(1, 3, 0, 2).reshape(G, N, FLAT)
        r2_parts += [v_re, -v_im]
        qr, qi = p_re[CHUNK], p_im[CHUNK]
        coefs += [jnp.concatenate([qr, qr], -1), jnp.concatenate([-qi, qi], -1),
                  jnp.concatenate([qi, -qi], -1)]
    r1 = jnp.concatenate([r1_parts[0][0], r1_parts[1][0], r1_parts[0][1], r1_parts[1][1]], axis=-1)
    r2 = jnp.concatenate(r2_parts, axis=1)
    coef = jnp.stack(coefs).reshape(6, G // GROUPS_PER_BLOCK, GROUPS_PER_BLOCK, 2 * N)
    return r1.astype(_BF16), r2.astype(_BF16), coef.transpose(1, 0, 2, 3)


def _mix_kernel(v_ref, vp_ref, vn_ref, zs_ref, ys_ref, gate_ref, h1_ref,
                cw_ref, cb_ref, lng_ref, lnb_ref, wpw_ref, d_ref, wglu_ref, wout_ref,
                h2_ref, buf_ref, *, tm):
    i = pl.program_id(1)
    last = pl.num_programs(1) - 1
    buf_ref[0:HALO, :] = jnp.where(i > 0, vp_ref[...], 0.0)
    buf_ref[HALO:HALO + tm, :] = v_ref[...]
    buf_ref[HALO + tm:2 * HALO + tm, :] = jnp.where(i < last, vn_ref[...], 0.0)
    acc = jnp.zeros((tm, CONV_WIDTH), _F32)
    for k in range(CONV_K):
        acc = acc + buf_ref[pl.ds(HALO - CONV_PAD + k, tm), :] * cw_ref[k:k + 1, :]
    c = acc + cb_ref[...]
    c = c - jnp.mean(c, axis=-1, keepdims=True)
    c = c * lax.rsqrt(jnp.mean(c * c, axis=-1, keepdims=True) + EPS) * lng_ref[...] + lnb_ref[...]
    conv_out = _dot((c * jax.nn.sigmoid(c)).astype(_BF16), wpw_ref[...])

    y = ys_ref[...] + d_ref[...] * zs_ref[...]
    ag = _dot(jax.nn.gelu(y).astype(_BF16), wglu_ref[...])
    ssm_out = ag[:, :D_MODEL] * jax.nn.sigmoid(ag[:, D_MODEL:])

    gate = gate_ref[...]
    merged = gate[:, :D_MODEL] * conv_out + gate[:, D_MODEL:] * ssm_out
    h2_ref[...] = h1_ref[...] + _dot(merged.astype(_BF16), wout_ref[...])


def _mix(v, zs, ys, gate, h1, cw, cb, lng, lnb, wpw, d, wglu, wout):
    bsz, length, _ = v.shape
    tm = ROW_TILE
    per = tm // HALO
    n_halo = length // HALO
    seq = lambda w: pl.BlockSpec((None, tm, w), lambda b, i: (b, i, 0))
    prev = pl.BlockSpec((None, HALO, CONV_WIDTH), lambda b, i: (b, jnp.maximum(i * per - 1, 0), 0))
    nxt = pl.BlockSpec((None, HALO, CONV_WIDTH),
                       lambda b, i: (b, jnp.minimum((i + 1) * per, n_halo - 1), 0))
    return pl.pallas_call(
        functools.partial(_mix_kernel, tm=tm),
        grid=(bsz, length // tm),
        in_specs=[seq(CONV_WIDTH), prev, nxt, seq(SSM_WIDTH), seq(SSM_WIDTH), seq(2 * D_MODEL),
                  seq(D_MODEL),
                  _const_spec(cw.shape), _const_spec((1, CONV_WIDTH)), _const_spec((1, CONV_WIDTH)),
                  _const_spec((1, CONV_WIDTH)), _const_spec(wpw.shape), _const_spec((1, SSM_WIDTH)),
                  _const_spec(wglu.shape), _const_spec(wout.shape)],
        out_specs=seq(D_MODEL),
        out_shape=jax.ShapeDtypeStruct((bsz, length, D_MODEL), _F32),
        scratch_shapes=[pltpu.VMEM((tm + 2 * HALO, CONV_WIDTH), _F32)],
        compiler_params=pltpu.CompilerParams(dimension_semantics=("arbitrary", "arbitrary"),
                                             vmem_limit_bytes=VMEM_LIMIT),
        name="mix",
    )(v, v, v, zs, ys, gate, h1, cw, cb, lng, lnb, wpw, d, wglu, wout)


def _mem_kv_kernel(mem_ref, g_ref, wkt_ref, wv_ref, kt_ref, v_ref):
    m = _rmsnorm(mem_ref[...], g_ref[...]).astype(_BF16)
    kt = lax.dot_general(wkt_ref[...], m, (((1,), (1,)), ((), ())), preferred_element_type=_F32)
    kt_ref[...] = kt.astype(_BF16)
    v_ref[...] = _dot(m, wv_ref[...]).astype(_BF16)


def _mem_kv(mem, g, wkt, wv):
    bsz = mem.shape[0]
    return pl.pallas_call(
        _mem_kv_kernel,
        grid=(bsz,),
        in_specs=[pl.BlockSpec((None, N_MEM, D_MODEL), lambda b: (b, 0, 0)),
                  _const_spec((1, D_MODEL)), _const_spec(wkt.shape), _const_spec(wv.shape)],
        out_specs=[pl.BlockSpec((None, D_MODEL, N_MEM), lambda b: (b, 0, 0)),
                   pl.BlockSpec((None, N_MEM, D_MODEL), lambda b: (b, 0, 0))],
        out_shape=[jax.ShapeDtypeStruct((bsz, D_MODEL, N_MEM), _BF16),
                   jax.ShapeDtypeStruct((bsz, N_MEM, D_MODEL), _BF16)],
        compiler_params=pltpu.CompilerParams(dimension_semantics=("arbitrary",)),
        name="mem_kv",
    )(mem, g, wkt, wv)


def _attn_ffn_kernel(h_ref, kt_ref, v_ref, gx_ref, wq_ref, wo_ref, g2_ref, wg_ref, wu_ref, wd_ref,
                     gf_ref, out_ref, act_ref, o_ref):
    h = h_ref[...]
    q = _dot(_rmsnorm(h, gx_ref[...]).astype(_BF16), wq_ref[...]).astype(_BF16)
    for hd in range(X_HEADS):
        sl = slice(hd * X_HEAD_DIM, (hd + 1) * X_HEAD_DIM)
        s = _dot(q[:, sl], kt_ref[sl, :]) * (X_HEAD_DIM ** -0.5)
        e = jnp.exp(s - jnp.max(s, axis=-1, keepdims=True))
        p = e / jnp.sum(e, axis=-1, keepdims=True)
        o_ref[:, sl] = _dot(p.astype(_BF16), v_ref[:, sl]).astype(_BF16)
    h = h + _dot(o_ref[...], wo_ref[...])
    xn = _rmsnorm(h, g2_ref[...]).astype(_BF16)
    h = h + 0.5 * _swiglu(xn, wg_ref, wu_ref, wd_ref, act_ref)
    out_ref[...] = _rmsnorm(h, gf_ref[...])


def _attn_ffn(h, kt, v, gx, wq, wo, g2, wg, wu, wd, gf):
    bsz, length, _ = h.shape
    tm = ROW_TILE
    seq = pl.BlockSpec((None, tm, D_MODEL), lambda b, i: (b, i, 0))
    return pl.pallas_call(
        _attn_ffn_kernel,
        grid=(bsz, length // tm),
        in_specs=[seq,
                  pl.BlockSpec((None, D_MODEL, N_MEM), lambda b, i: (b, 0, 0)),
                  pl.BlockSpec((None, N_MEM, D_MODEL), lambda b, i: (b, 0, 0)),
                  _const_spec((1, D_MODEL)), _const_spec(wq.shape), _const_spec(wo.shape),
                  _const_spec((1, D_MODEL)), _const_spec(wg.shape), _const_spec(wu.shape),
                  _const_spec(wd.shape), _const_spec((1, D_MODEL))],
        out_specs=seq,
        out_shape=jax.ShapeDtypeStruct(h.shape, _F32),
        scratch_shapes=[pltpu.VMEM((tm, D_FF), _BF16), pltpu.VMEM((tm, D_MODEL), _BF16)],
        compiler_params=pltpu.CompilerParams(dimension_semantics=("arbitrary", "arbitrary"),
                                             vmem_limit_bytes=VMEM_LIMIT),
        name="attn_ffn",
    )(h, kt, v, gx, wq, wo, g2, wg, wu, wd, gf)


def _split_gu(w_gu):
    wg = w_gu[:, :D_FF].reshape(D_MODEL, N_FF_CHUNKS, FF_CHUNK).transpose(1, 0, 2)
    wu = w_gu[:, D_FF:].reshape(D_MODEL, N_FF_CHUNKS, FF_CHUNK).transpose(1, 0, 2)
    return wg.astype(_BF16), wu.astype(_BF16)


def _row(vec):
    return vec.reshape(1, -1).astype(_F32)


def _trunk(x, mem, p):
    bsz, length, _ = x.shape
    n_chunks = length // CHUNK
    h1, v, zs, gate = _ffn_inproj(x.reshape(bsz * length, D_MODEL), p['ffn1_g'], p['ffn1_wg'],
                                  p['ffn1_wu'], p['ffn1_wd'], p['mix_g'], p['w_in'], p['b_in'])
    seq = lambda a: a.reshape(bsz, length, a.shape[-1])
    h1, v, zs, gate = seq(h1), seq(v), seq(zs), seq(gate)
    a = zs.astype(_BF16).reshape(bsz, n_chunks, CHUNK, SSM_GROUPS, SSM_GROUP)
    a = a.transpose(0, 3, 1, 2, 4).reshape(bsz, SSM_GROUPS, n_chunks, FLAT)
    ys = _ssm_scan(a, p['ssm_r1'], p['ssm_r2'], p['ssm_coef'])
    ys = ys.reshape(bsz, SSM_GROUPS, n_chunks, CHUNK, SSM_GROUP).transpose(0, 2, 3, 1, 4)
    ys = ys.reshape(bsz, length, SSM_WIDTH)
    h2 = _mix(v, zs, ys, gate, h1, p['conv_w'], p['conv_b'], p['conv_ln_g'], p['conv_ln_b'],
              p['conv_w_pw'], p['ssm_d'], p['ssm_w_glu'], p['w_out'])
    kt, vm = _mem_kv(mem, p['mem_g'], p['xattn_wkt'], p['xattn_wv'])
    return _attn_ffn(h2, kt, vm, p['xattn_g'], p['xattn_wq'], p['xattn_wo'], p['ffn2_g'],
                     p['ffn2_wg'], p['ffn2_wu'], p['ffn2_wd'], p['final_g'])


def kernel(x_prompt, x_sample, mem_prompt, mem_sample, ffn1_g, ffn1_wgu, ffn1_wd, mix_g, w_in, b_in, conv_w, conv_b, conv_ln_g, conv_ln_b, conv_w_pw, ssm_lam_re, ssm_lam_im, ssm_log_dt, ssm_b_re, ssm_b_im, ssm_c_re, ssm_c_im, ssm_d, ssm_w_glu, w_out, xattn_g, mem_g, xattn_wq, xattn_wkv, xattn_wo, ffn2_g, ffn2_wgu, ffn2_wd, final_g):
    assert ffn1_g.shape[0] == 1, "single-layer trunk"
    p = {}
    p['ffn1_g'], p['mix_g'], p['xattn_g'] = _row(ffn1_g[0]), _row(mix_g[0]), _row(xattn_g[0])
    p['mem_g'], p['ffn2_g'], p['final_g'] = _row(mem_g[0]), _row(ffn2_g[0]), _row(final_g)
    p['ffn1_wg'], p['ffn1_wu'] = _split_gu(ffn1_wgu[0])
    p['ffn2_wg'], p['ffn2_wu'] = _split_gu(ffn2_wgu[0])
    p['ffn1_wd'], p['ffn2_wd'] = ffn1_wd[0].astype(_BF16), ffn2_wd[0].astype(_BF16)
    p['w_in'], p['b_in'] = w_in[0].astype(_BF16), _row(b_in[0])
    p['conv_w'], p['conv_b'] = conv_w[0].astype(_F32), _row(conv_b[0])
    p['conv_ln_g'], p['conv_ln_b'] = _row(conv_ln_g[0]), _row(conv_ln_b[0])
    p['conv_w_pw'] = conv_w_pw[0].astype(_BF16)
    p['ssm_r1'], p['ssm_r2'], p['ssm_coef'] = _ssm_tables(
        ssm_lam_re[0], ssm_lam_im[0], ssm_log_dt[0], ssm_b_re[0], ssm_b_im[0],
        ssm_c_re[0], ssm_c_im[0])
    p['ssm_d'], p['ssm_w_glu'] = _row(ssm_d[0]), ssm_w_glu[0].astype(_BF16)
    p['w_out'] = w_out[0].astype(_BF16)
    p['xattn_wq'], p['xattn_wo'] = xattn_wq[0].astype(_BF16), xattn_wo[0].astype(_BF16)
    p['xattn_wkt'] = xattn_wkv[0][:, :D_MODEL].T.astype(_BF16)
    p['xattn_wv'] = xattn_wkv[0][:, D_MODEL:].astype(_BF16)
    return (_trunk(x_prompt, mem_prompt, p), _trunk(x_sample, mem_sample, p))
```

```python
import functools

import jax
import jax.numpy as jnp
from jax import lax
from jax.experimental import pallas as pl
from jax.experimental.pallas import tpu as pltpu

D_MODEL = 1024
D_FF = 2816
FF_CHUNK = 256
N_FF_CHUNKS = D_FF // FF_CHUNK
CONV_WIDTH = 512
CONV_K = 31
CONV_PAD = CONV_K // 2
SUBLANES = 8
LANES = 128
HALO = 2 * SUBLANES
CONV_ROWS = 64
CONV_COLS = 256
SSM_WIDTH = 512
SSM_GROUP = 16
SSM_GROUPS = SSM_WIDTH // SSM_GROUP
SSM_STATE = 64
CHUNK = 16
FLAT = CHUNK * SSM_GROUP
GROUPS_PER_BLOCK = LANES // SSM_GROUP
RELAYOUT_ROWS_IN = 64
RELAYOUT_ROWS_OUT = 32
SCAN_STEPS = 8
N_MEM = 256
X_HEADS = 4
X_HEAD_DIM = D_MODEL // X_HEADS
EPS = 1e-6
IN_COLS = 2 * CONV_WIDTH + SSM_WIDTH + 2 * D_MODEL

ROW_TILE = 512
FFN_ROW_TILE = 256
VMEM_LIMIT = 56 * 1024 * 1024
SCAN_VMEM_BUDGET = 44 * 1024 * 1024

_F32 = jnp.float32
_BF16 = jnp.bfloat16


def _dot(a, b):
    return jnp.dot(a, b, preferred_element_type=_F32)


def _rmsnorm(x, g):
    return x * lax.rsqrt(jnp.mean(x * x, axis=-1, keepdims=True) + EPS) * g


def _const_spec(shape):
    nd = len(shape)
    return pl.BlockSpec(shape, lambda *_: (0,) * nd, pipeline_mode=pl.Buffered(1))


def _swiglu(xn, wgu_ref, wd_ref, act_ref):
    for j in range(N_FF_CHUNKS):
        lo = j * FF_CHUNK
        g = _dot(xn, wgu_ref[:, lo:lo + FF_CHUNK])
        u = _dot(xn, wgu_ref[:, D_FF + lo:D_FF + lo + FF_CHUNK])
        act_ref[:, lo:lo + FF_CHUNK] = (g * jax.nn.sigmoid(g) * u).astype(_BF16)
    return _dot(act_ref[...], wd_ref[...])


def _ffn_inproj_kernel(x_ref, g1_ref, wgu_ref, wd_ref, gmix_ref, win_ref, bin_ref,
                       h1_ref, v_ref, zs_ref, gate_ref, act_ref):
    x = x_ref[...]
    xn = _rmsnorm(x, g1_ref[...]).astype(_BF16)
    h1 = x + 0.5 * _swiglu(xn, wgu_ref, wd_ref, act_ref)
    h1_ref[...] = h1
    un = _rmsnorm(h1, gmix_ref[...]).astype(_BF16)
    z = _dot(un, win_ref[...]) + bin_ref[...]
    v_ref[...] = z[:, :CONV_WIDTH] * jax.nn.sigmoid(z[:, CONV_WIDTH:2 * CONV_WIDTH])
    zs_ref[...] = z[:, 2 * CONV_WIDTH:2 * CONV_WIDTH + SSM_WIDTH]
    gate_ref[...] = jax.nn.sigmoid(z[:, 2 * CONV_WIDTH + SSM_WIDTH:])


def _ffn_inproj(x, g1, wgu, wd, gmix, win, b_in):
    rows = x.shape[0]
    tm = FFN_ROW_TILE
    row = lambda w: pl.BlockSpec((tm, w), lambda i: (i, 0))
    return pl.pallas_call(
        _ffn_inproj_kernel,
        grid=(rows // tm,),
        in_specs=[row(D_MODEL), _const_spec((1, D_MODEL)),
                  _const_spec(wgu.shape), _const_spec(wd.shape),
                  _const_spec((1, D_MODEL)), _const_spec(win.shape), _const_spec((1, IN_COLS))],
        out_specs=[row(D_MODEL), row(CONV_WIDTH), row(SSM_WIDTH), row(2 * D_MODEL)],
        out_shape=[jax.ShapeDtypeStruct((rows, D_MODEL), _F32),
                   jax.ShapeDtypeStruct((rows, CONV_WIDTH), _F32),
                   jax.ShapeDtypeStruct((rows, SSM_WIDTH), _F32),
                   jax.ShapeDtypeStruct((rows, 2 * D_MODEL), _F32)],
        scratch_shapes=[pltpu.VMEM((tm, D_FF), _BF16)],
        compiler_params=pltpu.CompilerParams(dimension_semantics=("arbitrary",),
                                             vmem_limit_bytes=VMEM_LIMIT),
        name="ffn_inproj",
    )(x, g1, wgu, wd, gmix, win, b_in)


def _piece_transpose(xs):
    xs = list(xs)
    piece = lax.broadcasted_iota(jnp.int32, xs[0].shape, 1) // SSM_GROUP
    for j in range(3):
        d = 1 << j
        keep = (piece & d) == 0
        for i in range(GROUPS_PER_BLOCK):
            if i & d:
                continue
            lo, hi = xs[i], xs[i + d]
            shift = SSM_GROUP * d
            if 2 * shift == LANES:
                both = pltpu.roll(jnp.where(keep, hi, lo), shift, 1)
                xs[i], xs[i + d] = jnp.where(keep, lo, both), jnp.where(keep, both, hi)
            else:
                xs[i] = jnp.where(keep, lo, pltpu.roll(hi, shift, 1))
                xs[i + d] = jnp.where(keep, pltpu.roll(lo, LANES - shift, 1), hi)
    return xs


def _ssm_scan_kernel(z_ref, w_ref, mv_ref, coef_ref, y_ref, a_ref, sf_ref, sb_ref, yy_ref,
                     *, n_chunks):
    nb = GROUPS_PER_BLOCK
    halves = CHUNK // nb

    def relayout_in(i, carry):
        rr = RELAYOUT_ROWS_IN
        row0 = pl.multiple_of(i * (rr * CHUNK), rr * CHUNK)
        crow = pl.multiple_of(i * rr, rr)
        for half in range(halves):
            xs = [pltpu.bitcast(z_ref[pl.ds(row0 + half * nb + t, rr, stride=CHUNK), :].astype(_BF16),
                                jnp.uint32) for t in range(nb)]
            xs = _piece_transpose(xs)
            for g in range(nb):
                a_ref[g, pl.ds(crow, rr), half * LANES:(half + 1) * LANES] = pltpu.bitcast(xs[g], _BF16)
        return carry

    lax.fori_loop(0, n_chunks // RELAYOUT_ROWS_IN, relayout_in, 0)

    for g in range(nb):
        s = _dot(a_ref[g], w_ref[g])
        packed = pl.ds(g, n_chunks, stride=nb)
        for d, s_ref in enumerate((sf_ref, sb_ref)):
            own = s[:, d * LANES:(d + 1) * LANES]
            s_ref[packed, :] = own
            yy_ref[d, packed, :] = pltpu.roll(own, SSM_STATE, 1)

    def scan(d, s_ref, a, b, bp, reverse):
        n_blocks = n_chunks // SCAN_STEPS
        span = SCAN_STEPS * nb

        def rows(blk):
            return pl.ds(pl.multiple_of(blk * span, span), span)

        def load(blk):
            return s_ref[rows(blk), :], yy_ref[d, rows(blk), :]

        def body(i, carry):
            v, w, s_all, sw_all = carry
            blk = n_blocks - 1 - i if reverse else i
            ahead = jnp.maximum(blk - 1, 0) if reverse else jnp.minimum(blk + 1, n_blocks - 1)
            s_next, sw_next = load(ahead)
            entering = [None] * SCAN_STEPS
            for k in (range(SCAN_STEPS - 1, -1, -1) if reverse else range(SCAN_STEPS)):
                entering[k] = v
                own = slice(k * nb, (k + 1) * nb)
                v, w = a * v + b * w + s_all[own], a * w + bp * v + sw_all[own]
            s_ref[rows(blk), :] = jnp.concatenate(entering, axis=0)
            return v, w, s_next, sw_next

        zero = jnp.zeros((nb, LANES), _F32)
        lax.fori_loop(0, n_blocks, body, (zero, zero) + load(n_blocks - 1 if reverse else 0))

    scan(0, sf_ref, coef_ref[0], coef_ref[1], coef_ref[2], reverse=False)
    scan(1, sb_ref, coef_ref[3], coef_ref[4], coef_ref[5], reverse=True)

    for g in range(nb):
        lhs = jnp.concatenate([a_ref[g],
                               sf_ref[pl.ds(g, n_chunks, stride=nb), :].astype(_BF16),
                               sb_ref[pl.ds(g, n_chunks, stride=nb), :].astype(_BF16)], axis=1)
        yg = _dot(lhs, mv_ref[g])
        for half in range(halves):
            yy_ref[half, g * n_chunks:(g + 1) * n_chunks, :] = yg[:, half * LANES:(half + 1) * LANES]

    def relayout_out(i, carry):
        rr = RELAYOUT_ROWS_OUT
        row0 = pl.multiple_of(i * (rr * CHUNK), rr * CHUNK)
        crow = pl.multiple_of(i * rr, rr)
        for half in range(halves):
            ys = _piece_transpose([yy_ref[half, pl.ds(g * n_chunks + crow, rr), :] for g in range(nb)])
            for t in range(nb):
                y_ref[pl.ds(row0 + half * nb + t, rr, stride=CHUNK), :] = ys[t]
        return carry

    lax.fori_loop(0, n_chunks // RELAYOUT_ROWS_OUT, relayout_out, 0)


def _ssm_scan(zs, w, mv, coef):
    bsz, length, _ = zs.shape
    n_chunks = length // CHUNK
    nb = GROUPS_PER_BLOCK
    block_bytes = length * LANES * 4
    scratch_bytes = n_chunks * nb * (FLAT * 2 + 2 * LANES * 4 + FLAT * 4)
    table_bytes = nb * (FLAT * FLAT + 2 * FLAT * FLAT) * 2
    double = 4 * block_bytes + scratch_bytes + table_bytes <= SCAN_VMEM_BUDGET
    seq_mode = pl.Buffered(2 if double else 1)
    once = pl.Buffered(1)
    seq_spec = pl.BlockSpec((None, length, LANES), lambda j, b: (b, 0, j), pipeline_mode=seq_mode)
    return pl.pallas_call(
        functools.partial(_ssm_scan_kernel, n_chunks=n_chunks),
        grid=(SSM_GROUPS // nb, bsz),
        in_specs=[seq_spec,
                  pl.BlockSpec((nb, FLAT, FLAT), lambda j, b: (j, 0, 0), pipeline_mode=once),
                  pl.BlockSpec((nb, 2 * FLAT, FLAT), lambda j, b: (j, 0, 0), pipeline_mode=once),
                  pl.BlockSpec((None, 6, nb, LANES), lambda j, b: (j, 0, 0, 0), pipeline_mode=once)],
        out_specs=seq_spec,
        out_shape=jax.ShapeDtypeStruct(zs.shape, _F32),
        scratch_shapes=[pltpu.VMEM((nb, n_chunks, FLAT), _BF16),
                        pltpu.VMEM((n_chunks * nb, LANES), _F32),
                        pltpu.VMEM((n_chunks * nb, LANES), _F32),
                        pltpu.VMEM((2, n_chunks * nb, LANES), _F32)],
        compiler_params=pltpu.CompilerParams(dimension_semantics=("arbitrary", "arbitrary"),
                                             vmem_limit_bytes=VMEM_LIMIT),
        name="ssm_scan",
    )(zs, w, mv, coef)


def _ssm_direction_tables(lam_re, lam_im, log_dt, b_re, b_im, c_re, c_im):
    dt = jnp.exp(log_dt)[:, None]
    mag = jnp.exp(lam_re * dt)
    lr, li = mag * jnp.cos(lam_im * dt), mag * jnp.sin(lam_im * dt)
    den = lam_re * lam_re + lam_im * lam_im
    qr = ((lr - 1.0) * lam_re + li * lam_im) / den
    qi = (li * lam_re - (lr - 1.0) * lam_im) / den
    bb_re = qr[..., None] * b_re - qi[..., None] * b_im
    bb_im = qr[..., None] * b_im + qi[..., None] * b_re
    p_re, p_im = [jnp.ones_like(lr)], [jnp.zeros_like(lr)]
    for _ in range(CHUNK):
        p_re, p_im = (p_re + [p_re[-1] * lr - p_im[-1] * li],
                      p_im + [p_re[-1] * li + p_im[-1] * lr])
    p_re, p_im = jnp.stack(p_re), jnp.stack(p_im)
    cp_re = c_re[None] * p_re[:, :, None, :] - c_im[None] * p_im[:, :, None, :]
    cp_im = c_re[None] * p_im[:, :, None, :] + c_im[None] * p_re[:, :, None, :]
    kern = (jnp.einsum('kgpn,gnh->kgph', cp_re[:CHUNK], bb_re, precision=lax.Precision.HIGHEST)
            - jnp.einsum('kgpn,gnh->kgph', cp_im[:CHUNK], bb_im, precision=lax.Precision.HIGHEST))
    return kern, p_re, p_im, bb_re, bb_im, cp_re, cp_im


def _ssm_tables(lam_re, lam_im, log_dt, b_re, b_im, c_re, c_im):
    G, N = SSM_GROUPS, SSM_STATE
    lag = jnp.arange(CHUNK)[None, :] - jnp.arange(CHUNK)[:, None]
    toeplitz, w_parts, v_parts, coefs = 0.0, [], [], []
    for d in range(2):
        kern, p_re, p_im, bb_re, bb_im, cp_re, cp_im = _ssm_direction_tables(
            lam_re[d], lam_im[d], log_dt[d], b_re[d], b_im[d], c_re[d], c_im[d])
        rel = lag if d == 0 else -lag
        toe = jnp.where((rel >= 0)[:, :, None, None, None],
                        kern[jnp.clip(rel, 0, CHUNK - 1)], 0.0)
        toeplitz = toeplitz + toe.transpose(2, 0, 4, 1, 3).reshape(G, FLAT, FLAT)
        pw = jnp.arange(CHUNK - 1, -1, -1) if d == 0 else jnp.arange(CHUNK)
        w_re = p_re[pw][:, :, :, None] * bb_re[None] - p_im[pw][:, :, :, None] * bb_im[None]
        w_im = p_re[pw][:, :, :, None] * bb_im[None] + p_im[pw][:, :, :, None] * bb_re[None]
        w_parts += [w_re.transpose(1, 0, 3, 2).reshape(G, FLAT, N),
                    w_im.transpose(1, 0, 3, 2).reshape(G, FLAT, N)]
        po = jnp.arange(1, CHUNK + 1) if d == 0 else jnp.arange(CHUNK, 0, -1)
        v_parts += [cp_re[po].transpose(1, 3, 0, 2).reshape(G, N, FLAT),
                    -cp_im[po].transpose(1, 3, 0, 2).reshape(G, N, FLAT)]
        qr, qi = p_re[CHUNK], p_im[CHUNK]
        coefs += [jnp.concatenate([qr, qr], -1), jnp.concatenate([-qi, qi], -1),
                  jnp.concatenate([qi, -qi], -1)]
    w = jnp.concatenate(w_parts, axis=-1)
    mv = jnp.concatenate([toeplitz] + v_parts, axis=1)
    coef = jnp.stack(coefs).reshape(6, G // GROUPS_PER_BLOCK, GROUPS_PER_BLOCK, 2 * N)
    return w.astype(_BF16), mv.astype(_BF16), coef.transpose(1, 0, 2, 3)


def _depthwise_conv(buf_ref, cw_ref, c_ref, tm):
    def rows(i, carry):
        r0 = pl.multiple_of(i * CONV_ROWS, CONV_ROWS)
        for c0 in range(0, CONV_WIDTH, CONV_COLS):
            cols = slice(c0, c0 + CONV_COLS)
            out = jnp.zeros((CONV_ROWS, CONV_COLS), _F32)
            for r in range(SUBLANES):
                part = None
                for m in range(HALO * 2 // SUBLANES):
                    k = SUBLANES * m + r - 1
                    if 0 <= k < CONV_K:
                        term = (buf_ref[pl.ds(r0 + SUBLANES * m, CONV_ROWS + SUBLANES), cols]
                                * cw_ref[k:k + 1, cols])
                        part = term if part is None else part + term
                out = out + part[r:r + CONV_ROWS]
            c_ref[pl.ds(r0, CONV_ROWS), cols] = out
        return carry

    lax.fori_loop(0, tm // CONV_ROWS, rows, 0)


def _mix_kernel(v_ref, vp_ref, vn_ref, zs_ref, ys_ref, gate_ref, h1_ref,
                cw_ref, cb_ref, lng_ref, lnb_ref, wpw_ref, d_ref, wglu_ref, wout_ref,
                h2_ref, buf_ref, c_ref, *, tm):
    i = pl.program_id(1)
    last = pl.num_programs(1) - 1
    buf_ref[0:HALO, :] = jnp.where(i > 0, vp_ref[...], 0.0)
    buf_ref[HALO:HALO + tm, :] = v_ref[...]
    buf_ref[HALO + tm:2 * HALO + tm, :] = jnp.where(i < last, vn_ref[...], 0.0)
    _depthwise_conv(buf_ref, cw_ref, c_ref, tm)
    c = c_ref[...] + cb_ref[...]
    c = c - jnp.mean(c, axis=-1, keepdims=True)
    c = c * lax.rsqrt(jnp.mean(c * c, axis=-1, keepdims=True) + EPS) * lng_ref[...] + lnb_ref[...]
    conv_out = _dot((c * jax.nn.sigmoid(c)).astype(_BF16), wpw_ref[...])

    y = ys_ref[...] + d_ref[...] * zs_ref[...]
    ag = _dot(jax.nn.gelu(y).astype(_BF16), wglu_ref[...])
    ssm_out = ag[:, :D_MODEL] * jax.nn.sigmoid(ag[:, D_MODEL:])

    gate = gate_ref[...]
    merged = gate[:, :D_MODEL] * conv_out + gate[:, D_MODEL:] * ssm_out
    h2_ref[...] = h1_ref[...] + _dot(merged.astype(_BF16), wout_ref[...])


def _mix(v, zs, ys, gate, h1, cw, cb, lng, lnb, wpw, d, wglu, wout):
    bsz, length, _ = v.shape
    tm = ROW_TILE
    per = tm // HALO
    n_halo = length // HALO
    seq = lambda w: pl.BlockSpec((None, tm, w), lambda b, i: (b, i, 0))
    prev = pl.BlockSpec((None, HALO, CONV_WIDTH), lambda b, i: (b, jnp.maximum(i * per - 1, 0), 0))
    nxt = pl.BlockSpec((None, HALO, CONV_WIDTH),
                       lambda b, i: (b, jnp.minimum((i + 1) * per, n_halo - 1), 0))
    return pl.pallas_call(
        functools.partial(_mix_kernel, tm=tm),
        grid=(bsz, length // tm),
        in_specs=[seq(CONV_WIDTH), prev, nxt, seq(SSM_WIDTH), seq(SSM_WIDTH), seq(2 * D_MODEL),
                  seq(D_MODEL),
                  _const_spec(cw.shape), _const_spec((1, CONV_WIDTH)), _const_spec((1, CONV_WIDTH)),
                  _const_spec((1, CONV_WIDTH)), _const_spec(wpw.shape), _const_spec((1, SSM_WIDTH)),
                  _const_spec(wglu.shape), _const_spec(wout.shape)],
        out_specs=seq(D_MODEL),
        out_shape=jax.ShapeDtypeStruct((bsz, length, D_MODEL), _F32),
        scratch_shapes=[pltpu.VMEM((tm + 2 * HALO, CONV_WIDTH), _F32),
                        pltpu.VMEM((tm, CONV_WIDTH), _F32)],
        compiler_params=pltpu.CompilerParams(dimension_semantics=("arbitrary", "arbitrary"),
                                             vmem_limit_bytes=VMEM_LIMIT),
        name="mix",
    )(v, v, v, zs, ys, gate, h1, cw, cb, lng, lnb, wpw, d, wglu, wout)


def _mem_kv_kernel(mem_ref, g_ref, wkt_ref, wv_ref, kt_ref, v_ref):
    m = _rmsnorm(mem_ref[...], g_ref[...]).astype(_BF16)
    kt = lax.dot_general(wkt_ref[...], m, (((1,), (1,)), ((), ())), preferred_element_type=_F32)
    kt_ref[...] = kt.astype(_BF16)
    v_ref[...] = _dot(m, wv_ref[...]).astype(_BF16)


def _mem_kv(mem, g, wkt, wv):
    bsz = mem.shape[0]
    return pl.pallas_call(
        _mem_kv_kernel,
        grid=(bsz,),
        in_specs=[pl.BlockSpec((None, N_MEM, D_MODEL), lambda b: (b, 0, 0)),
                  _const_spec((1, D_MODEL)), _const_spec(wkt.shape), _const_spec(wv.shape)],
        out_specs=[pl.BlockSpec((None, D_MODEL, N_MEM), lambda b: (b, 0, 0)),
                   pl.BlockSpec((None, N_MEM, D_MODEL), lambda b: (b, 0, 0))],
        out_shape=[jax.ShapeDtypeStruct((bsz, D_MODEL, N_MEM), _BF16),
                   jax.ShapeDtypeStruct((bsz, N_MEM, D_MODEL), _BF16)],
        compiler_params=pltpu.CompilerParams(dimension_semantics=("arbitrary",)),
        name="mem_kv",
    )(mem, g, wkt, wv)


def _attn_ffn_kernel(h_ref, kt_ref, v_ref, gx_ref, wq_ref, wo_ref, g2_ref, wgu_ref, wd_ref,
                     gf_ref, out_ref, act_ref, o_ref):
    h = h_ref[...]
    q = _dot(_rmsnorm(h, gx_ref[...]).astype(_BF16), wq_ref[...]).astype(_BF16)
    for hd in range(X_HEADS):
        sl = slice(hd * X_HEAD_DIM, (hd + 1) * X_HEAD_DIM)
        s = _dot(q[:, sl], kt_ref[sl, :]) * (X_HEAD_DIM ** -0.5)
        e = jnp.exp(s - jnp.max(s, axis=-1, keepdims=True))
        p = e / jnp.sum(e, axis=-1, keepdims=True)
        o_ref[:, sl] = _dot(p.astype(_BF16), v_ref[:, sl]).astype(_BF16)
    h = h + _dot(o_ref[...], wo_ref[...])
    xn = _rmsnorm(h, g2_ref[...]).astype(_BF16)
    h = h + 0.5 * _swiglu(xn, wgu_ref, wd_ref, act_ref)
    out_ref[...] = _rmsnorm(h, gf_ref[...])


def _attn_ffn(h, kt, v, gx, wq, wo, g2, wgu, wd, gf):
    bsz, length, _ = h.shape
    tm = ROW_TILE
    seq = pl.BlockSpec((None, tm, D_MODEL), lambda b, i: (b, i, 0))
    return pl.pallas_call(
        _attn_ffn_kernel,
        grid=(bsz, length // tm),
        in_specs=[seq,
                  pl.BlockSpec((None, D_MODEL, N_MEM), lambda b, i: (b, 0, 0)),
                  pl.BlockSpec((None, N_MEM, D_MODEL), lambda b, i: (b, 0, 0)),
                  _const_spec((1, D_MODEL)), _const_spec(wq.shape), _const_spec(wo.shape),
                  _const_spec((1, D_MODEL)), _const_spec(wgu.shape), _const_spec(wd.shape),
                  _const_spec((1, D_MODEL))],
        out_specs=seq,
        out_shape=jax.ShapeDtypeStruct(h.shape, _F32),
        scratch_shapes=[pltpu.VMEM((tm, D_FF), _BF16), pltpu.VMEM((tm, D_MODEL), _BF16)],
        compiler_params=pltpu.CompilerParams(dimension_semantics=("arbitrary", "arbitrary"),
                                             vmem_limit_bytes=VMEM_LIMIT),
        name="attn_ffn",
    )(h, kt, v, gx, wq, wo, g2, wgu, wd, gf)


def _row(vec):
    return vec.reshape(1, -1).astype(_F32)


def _trunk(x, mem, p):
    bsz, length, _ = x.shape
    h1, v, zs, gate = _ffn_inproj(x.reshape(bsz * length, D_MODEL), p['ffn1_g'], p['ffn1_wgu'],
                                  p['ffn1_wd'], p['mix_g'], p['w_in'], p['b_in'])
    seq = lambda a: a.reshape(bsz, length, a.shape[-1])
    h1, v, zs, gate = seq(h1), seq(v), seq(zs), seq(gate)
    ys = _ssm_scan(zs, p['ssm_w'], p['ssm_mv'], p['ssm_coef'])
    h2 = _mix(v, zs, ys, gate, h1, p['conv_w'], p['conv_b'], p['conv_ln_g'], p['conv_ln_b'],
              p['conv_w_pw'], p['ssm_d'], p['ssm_w_glu'], p['w_out'])
    kt, vm = _mem_kv(mem, p['mem_g'], p['xattn_wkt'], p['xattn_wv'])
    return _attn_ffn(h2, kt, vm, p['xattn_g'], p['xattn_wq'], p['xattn_wo'], p['ffn2_g'],
                     p['ffn2_wgu'], p['ffn2_wd'], p['final_g'])


def kernel(x_prompt, x_sample, mem_prompt, mem_sample, ffn1_g, ffn1_wgu, ffn1_wd, mix_g, w_in, b_in, conv_w, conv_b, conv_ln_g, conv_ln_b, conv_w_pw, ssm_lam_re, ssm_lam_im, ssm_log_dt, ssm_b_re, ssm_b_im, ssm_c_re, ssm_c_im, ssm_d, ssm_w_glu, w_out, xattn_g, mem_g, xattn_wq, xattn_wkv, xattn_wo, ffn2_g, ffn2_wgu, ffn2_wd, final_g):
    assert ffn1_g.shape[0] == 1, "single-layer trunk"
    p = {}
    p['ffn1_g'], p['mix_g'], p['xattn_g'] = _row(ffn1_g[0]), _row(mix_g[0]), _row(xattn_g[0])
    p['mem_g'], p['ffn2_g'], p['final_g'] = _row(mem_g[0]), _row(ffn2_g[0]), _row(final_g)
    p['ffn1_wgu'], p['ffn2_wgu'] = ffn1_wgu[0].astype(_BF16), ffn2_wgu[0].astype(_BF16)
    p['ffn1_wd'], p['ffn2_wd'] = ffn1_wd[0].astype(_BF16), ffn2_wd[0].astype(_BF16)
    p['w_in'], p['b_in'] = w_in[0].astype(_BF16), _row(b_in[0])
    p['conv_w'], p['conv_b'] = conv_w[0].astype(_F32), _row(conv_b[0])
    p['conv_ln_g'], p['conv_ln_b'] = _row(conv_ln_g[0]), _row(conv_ln_b[0])
    p['conv_w_pw'] = conv_w_pw[0].astype(_BF16)
    p['ssm_w'], p['ssm_mv'], p['ssm_coef'] = _ssm_tables(
        ssm_lam_re[0], ssm_lam_im[0], ssm_log_dt[0], ssm_b_re[0], ssm_b_im[0],
        ssm_c_re[0], ssm_c_im[0])
    p['ssm_d'], p['ssm_w_glu'] = _row(ssm_d[0]), ssm_w_glu[0].astype(_BF16)
    p['w_out'] = w_out[0].astype(_BF16)
    p['xattn_wq'], p['xattn_wo'] = xattn_wq[0].astype(_BF16), xattn_wo[0].astype(_BF16)
    p['xattn_wkt'] = xattn_wkv[0][:, :D_MODEL].T.astype(_BF16)
    p['xattn_wv'] = xattn_wkv[0][:, D_MODEL:].astype(_BF16)
    return (_trunk(x_prompt, mem_prompt, p), _trunk(x_sample, mem_sample, p))
```

```python
import functools

import jax
import jax.numpy as jnp
from jax import lax
from jax.experimental import pallas as pl
from jax.experimental.pallas import tpu as pltpu

D_MODEL = 1024
D_FF = 2816
FF_CHUNK = 256
N_FF_CHUNKS = D_FF // FF_CHUNK
CONV_WIDTH = 512
CONV_K = 31
CONV_PAD = CONV_K // 2
SUBLANES = 8
LANES = 128
HALO = 2 * SUBLANES
CONV_ROWS = 64
CONV_COLS = 256
SSM_WIDTH = 512
SSM_GROUP = 16
SSM_GROUPS = SSM_WIDTH // SSM_GROUP
SSM_STATE = 64
CHUNK = 16
FLAT = CHUNK * SSM_GROUP
GROUPS_PER_BLOCK = LANES // SSM_GROUP
RELAYOUT_ROWS_IN = 64
RELAYOUT_ROWS_OUT = 32
SCAN_STEPS = 8
N_MEM = 256
X_HEADS = 4
X_HEAD_DIM = D_MODEL // X_HEADS
EPS = 1e-6
IN_COLS = 2 * CONV_WIDTH + SSM_WIDTH + 2 * D_MODEL

FFN_ROW_TILE = 256
POST_ROW_TILE = 256
VMEM_LIMIT = 56 * 1024 * 1024
SCAN_VMEM_BUDGET = 44 * 1024 * 1024

_F32 = jnp.float32
_BF16 = jnp.bfloat16


def _dot(a, b):
    return jnp.dot(a, b, preferred_element_type=_F32)


def _rmsnorm(x, g):
    return x * lax.rsqrt(jnp.mean(x * x, axis=-1, keepdims=True) + EPS) * g


def _const_spec(shape):
    nd = len(shape)
    return pl.BlockSpec(shape, lambda *_: (0,) * nd, pipeline_mode=pl.Buffered(1))


def _swiglu(xn, wgu_ref, wd_ref, act_ref, side_work=()):
    assert len(side_work) <= N_FF_CHUNKS
    for j in range(N_FF_CHUNKS):
        lo = j * FF_CHUNK
        g = _dot(xn, wgu_ref[:, lo:lo + FF_CHUNK])
        u = _dot(xn, wgu_ref[:, D_FF + lo:D_FF + lo + FF_CHUNK])
        act_ref[:, lo:lo + FF_CHUNK] = (g * jax.nn.sigmoid(g) * u).astype(_BF16)
        if j < len(side_work):
            side_work[j](g)
    return _dot(act_ref[...], wd_ref[...])


def _zeros_after(x):
    bits = lax.bitcast_convert_type(x, jnp.uint32)
    return lax.bitcast_convert_type((bits >> 16) >> 16, _F32)


def _ffn_inproj_kernel(x_ref, g1_ref, wgu_ref, wd_ref, gmix_ref, win_ref, bin_ref,
                       h1_ref, v_ref, zs_ref, gate_ref, act_ref):
    x = x_ref[...]
    xn = _rmsnorm(x, g1_ref[...]).astype(_BF16)
    h1 = x + 0.5 * _swiglu(xn, wgu_ref, wd_ref, act_ref)
    h1_ref[...] = h1
    un = _rmsnorm(h1, gmix_ref[...]).astype(_BF16)
    z = _dot(un, win_ref[...]) + bin_ref[...]
    v_ref[...] = z[:, :CONV_WIDTH] * jax.nn.sigmoid(z[:, CONV_WIDTH:2 * CONV_WIDTH])
    zs_ref[...] = z[:, 2 * CONV_WIDTH:2 * CONV_WIDTH + SSM_WIDTH]
    gate_ref[...] = jax.nn.sigmoid(z[:, 2 * CONV_WIDTH + SSM_WIDTH:])


def _ffn_inproj(x, g1, wgu, wd, gmix, win, b_in):
    rows = x.shape[0]
    tm = FFN_ROW_TILE
    row = lambda w: pl.BlockSpec((tm, w), lambda i: (i, 0))
    return pl.pallas_call(
        _ffn_inproj_kernel,
        grid=(rows // tm,),
        in_specs=[row(D_MODEL), _const_spec((1, D_MODEL)),
                  _const_spec(wgu.shape), _const_spec(wd.shape),
                  _const_spec((1, D_MODEL)), _const_spec(win.shape), _const_spec((1, IN_COLS))],
        out_specs=[row(D_MODEL), row(CONV_WIDTH), row(SSM_WIDTH), row(2 * D_MODEL)],
        out_shape=[jax.ShapeDtypeStruct((rows, D_MODEL), _F32),
                   jax.ShapeDtypeStruct((rows, CONV_WIDTH), _F32),
                   jax.ShapeDtypeStruct((rows, SSM_WIDTH), _F32),
                   jax.ShapeDtypeStruct((rows, 2 * D_MODEL), _F32)],
        scratch_shapes=[pltpu.VMEM((tm, D_FF), _BF16)],
        compiler_params=pltpu.CompilerParams(dimension_semantics=("arbitrary",),
                                             vmem_limit_bytes=VMEM_LIMIT),
        name="ffn_inproj",
    )(x, g1, wgu, wd, gmix, win, b_in)


def _piece_transpose(xs):
    xs = list(xs)
    piece = lax.broadcasted_iota(jnp.int32, xs[0].shape, 1) // SSM_GROUP
    for j in range(3):
        d = 1 << j
        keep = (piece & d) == 0
        for i in range(GROUPS_PER_BLOCK):
            if i & d:
                continue
            lo, hi = xs[i], xs[i + d]
            shift = SSM_GROUP * d
            if 2 * shift == LANES:
                both = pltpu.roll(jnp.where(keep, hi, lo), shift, 1)
                xs[i], xs[i + d] = jnp.where(keep, lo, both), jnp.where(keep, both, hi)
            else:
                xs[i] = jnp.where(keep, lo, pltpu.roll(hi, shift, 1))
                xs[i + d] = jnp.where(keep, pltpu.roll(lo, LANES - shift, 1), hi)
    return xs


def _ssm_scan_kernel(z_ref, w_ref, mv_ref, coef_ref, y_ref, a_ref, sf_ref, sb_ref, yy_ref,
                     *, n_chunks):
    nb = GROUPS_PER_BLOCK
    halves = CHUNK // nb

    def relayout_in(i, carry):
        rr = RELAYOUT_ROWS_IN
        row0 = pl.multiple_of(i * (rr * CHUNK), rr * CHUNK)
        crow = pl.multiple_of(i * rr, rr)
        for half in range(halves):
            xs = [pltpu.bitcast(z_ref[pl.ds(row0 + half * nb + t, rr, stride=CHUNK), :].astype(_BF16),
                                jnp.uint32) for t in range(nb)]
            xs = _piece_transpose(xs)
            for g in range(nb):
                a_ref[g, pl.ds(crow, rr), half * LANES:(half + 1) * LANES] = pltpu.bitcast(xs[g], _BF16)
        return carry

    lax.fori_loop(0, n_chunks // RELAYOUT_ROWS_IN, relayout_in, 0)

    for g in range(nb):
        s = _dot(a_ref[g], w_ref[g])
        packed = pl.ds(g, n_chunks, stride=nb)
        for d, s_ref in enumerate((sf_ref, sb_ref)):
            own = s[:, d * LANES:(d + 1) * LANES]
            s_ref[packed, :] = own
            yy_ref[d, packed, :] = pltpu.roll(own, SSM_STATE, 1)

    def scan(d, s_ref, a, b, bp, reverse):
        n_blocks = n_chunks // SCAN_STEPS
        span = SCAN_STEPS * nb

        def rows(blk):
            return pl.ds(pl.multiple_of(blk * span, span), span)

        def load(blk):
            return s_ref[rows(blk), :], yy_ref[d, rows(blk), :]

        def body(i, carry):
            v, w, s_all, sw_all = carry
            blk = n_blocks - 1 - i if reverse else i
            ahead = jnp.maximum(blk - 1, 0) if reverse else jnp.minimum(blk + 1, n_blocks - 1)
            s_next, sw_next = load(ahead)
            entering = [None] * SCAN_STEPS
            for k in (range(SCAN_STEPS - 1, -1, -1) if reverse else range(SCAN_STEPS)):
                entering[k] = v
                own = slice(k * nb, (k + 1) * nb)
                v, w = a * v + b * w + s_all[own], a * w + bp * v + sw_all[own]
            s_ref[rows(blk), :] = jnp.concatenate(entering, axis=0)
            return v, w, s_next, sw_next

        zero = jnp.zeros((nb, LANES), _F32)
        lax.fori_loop(0, n_blocks, body, (zero, zero) + load(n_blocks - 1 if reverse else 0))

    scan(0, sf_ref, coef_ref[0], coef_ref[1], coef_ref[2], reverse=False)
    scan(1, sb_ref, coef_ref[3], coef_ref[4], coef_ref[5], reverse=True)

    for g in range(nb):
        lhs = jnp.concatenate([a_ref[g],
                               sf_ref[pl.ds(g, n_chunks, stride=nb), :].astype(_BF16),
                               sb_ref[pl.ds(g, n_chunks, stride=nb), :].astype(_BF16)], axis=1)
        yg = _dot(lhs, mv_ref[g])
        for half in range(halves):
            yy_ref[half, g * n_chunks:(g + 1) * n_chunks, :] = yg[:, half * LANES:(half + 1) * LANES]

    def relayout_out(i, carry):
        rr = RELAYOUT_ROWS_OUT
        row0 = pl.multiple_of(i * (rr * CHUNK), rr * CHUNK)
        crow = pl.multiple_of(i * rr, rr)
        for half in range(halves):
            ys = _piece_transpose([yy_ref[half, pl.ds(g * n_chunks + crow, rr), :] for g in range(nb)])
            for t in range(nb):
                y_ref[pl.ds(row0 + half * nb + t, rr, stride=CHUNK), :] = ys[t]
        return carry

    lax.fori_loop(0, n_chunks // RELAYOUT_ROWS_OUT, relayout_out, 0)


def _ssm_scan(zs, w, mv, coef):
    bsz, length, _ = zs.shape
    n_chunks = length // CHUNK
    nb = GROUPS_PER_BLOCK
    block_bytes = length * LANES * 4
    scratch_bytes = n_chunks * nb * (FLAT * 2 + 2 * LANES * 4 + FLAT * 4)
    table_bytes = nb * (FLAT * FLAT + 2 * FLAT * FLAT) * 2
    double = 4 * block_bytes + scratch_bytes + table_bytes <= SCAN_VMEM_BUDGET
    seq_mode = pl.Buffered(2 if double else 1)
    once = pl.Buffered(1)
    seq_spec = pl.BlockSpec((None, length, LANES), lambda j, b: (b, 0, j), pipeline_mode=seq_mode)
    return pl.pallas_call(
        functools.partial(_ssm_scan_kernel, n_chunks=n_chunks),
        grid=(SSM_GROUPS // nb, bsz),
        in_specs=[seq_spec,
                  pl.BlockSpec((nb, FLAT, FLAT), lambda j, b: (j, 0, 0), pipeline_mode=once),
                  pl.BlockSpec((nb, 2 * FLAT, FLAT), lambda j, b: (j, 0, 0), pipeline_mode=once),
                  pl.BlockSpec((None, 6, nb, LANES), lambda j, b: (j, 0, 0, 0), pipeline_mode=once)],
        out_specs=seq_spec,
        out_shape=jax.ShapeDtypeStruct(zs.shape, _F32),
        scratch_shapes=[pltpu.VMEM((nb, n_chunks, FLAT), _BF16),
                        pltpu.VMEM((n_chunks * nb, LANES), _F32),
                        pltpu.VMEM((n_chunks * nb, LANES), _F32),
                        pltpu.VMEM((2, n_chunks * nb, LANES), _F32)],
        compiler_params=pltpu.CompilerParams(dimension_semantics=("arbitrary", "arbitrary"),
                                             vmem_limit_bytes=VMEM_LIMIT),
        name="ssm_scan",
    )(zs, w, mv, coef)


def _ssm_direction_tables(lam_re, lam_im, log_dt, b_re, b_im, c_re, c_im):
    dt = jnp.exp(log_dt)[:, None]
    mag = jnp.exp(lam_re * dt)
    lr, li = mag * jnp.cos(lam_im * dt), mag * jnp.sin(lam_im * dt)
    den = lam_re * lam_re + lam_im * lam_im
    qr = ((lr - 1.0) * lam_re + li * lam_im) / den
    qi = (li * lam_re - (lr - 1.0) * lam_im) / den
    bb_re = qr[..., None] * b_re - qi[..., None] * b_im
    bb_im = qr[..., None] * b_im + qi[..., None] * b_re
    p_re, p_im = [jnp.ones_like(lr)], [jnp.zeros_like(lr)]
    for _ in range(CHUNK):
        p_re, p_im = (p_re + [p_re[-1] * lr - p_im[-1] * li],
                      p_im + [p_re[-1] * li + p_im[-1] * lr])
    p_re, p_im = jnp.stack(p_re), jnp.stack(p_im)
    cp_re = c_re[None] * p_re[:, :, None, :] - c_im[None] * p_im[:, :, None, :]
    cp_im = c_re[None] * p_im[:, :, None, :] + c_im[None] * p_re[:, :, None, :]
    kern = (jnp.einsum('kgpn,gnh->kgph', cp_re[:CHUNK], bb_re, precision=lax.Precision.HIGHEST)
            - jnp.einsum('kgpn,gnh->kgph', cp_im[:CHUNK], bb_im, precision=lax.Precision.HIGHEST))
    return kern, p_re, p_im, bb_re, bb_im, cp_re, cp_im


def _ssm_tables(lam_re, lam_im, log_dt, b_re, b_im, c_re, c_im):
    G, N = SSM_GROUPS, SSM_STATE
    lag = jnp.arange(CHUNK)[None, :] - jnp.arange(CHUNK)[:, None]
    toeplitz, w_parts, v_parts, coefs = 0.0, [], [], []
    for d in range(2):
        kern, p_re, p_im, bb_re, bb_im, cp_re, cp_im = _ssm_direction_tables(
            lam_re[d], lam_im[d], log_dt[d], b_re[d], b_im[d], c_re[d], c_im[d])
        rel = lag if d == 0 else -lag
        toe = jnp.where((rel >= 0)[:, :, None, None, None],
                        kern[jnp.clip(rel, 0, CHUNK - 1)], 0.0)
        toeplitz = toeplitz + toe.transpose(2, 0, 4, 1, 3).reshape(G, FLAT, FLAT)
        pw = jnp.arange(CHUNK - 1, -1, -1) if d == 0 else jnp.arange(CHUNK)
        w_re = p_re[pw][:, :, :, None] * bb_re[None] - p_im[pw][:, :, :, None] * bb_im[None]
        w_im = p_re[pw][:, :, :, None] * bb_im[None] + p_im[pw][:, :, :, None] * bb_re[None]
        w_parts += [w_re.transpose(1, 0, 3, 2).reshape(G, FLAT, N),
                    w_im.transpose(1, 0, 3, 2).reshape(G, FLAT, N)]
        po = jnp.arange(1, CHUNK + 1) if d == 0 else jnp.arange(CHUNK, 0, -1)
        v_parts += [cp_re[po].transpose(1, 3, 0, 2).reshape(G, N, FLAT),
                    -cp_im[po].transpose(1, 3, 0, 2).reshape(G, N, FLAT)]
        qr, qi = p_re[CHUNK], p_im[CHUNK]
        coefs += [jnp.concatenate([qr, qr], -1), jnp.concatenate([-qi, qi], -1),
                  jnp.concatenate([qi, -qi], -1)]
    w = jnp.concatenate(w_parts, axis=-1)
    mv = jnp.concatenate([toeplitz] + v_parts, axis=1)
    coef = jnp.stack(coefs).reshape(6, G // GROUPS_PER_BLOCK, GROUPS_PER_BLOCK, 2 * N)
    return w.astype(_BF16), mv.astype(_BF16), coef.transpose(1, 0, 2, 3)


def _stage_conv_input(buf_ref, before, rows, after, tm):
    buf_ref[0:HALO, :] = before
    buf_ref[HALO:HALO + tm, :] = rows
    buf_ref[HALO + tm:2 * HALO + tm, :] = after


def _depthwise_conv_pieces(buf_ref, cw_ref, c_ref, tm):
    def piece(r0, c0, after=None):
        cols = slice(c0, c0 + CONV_COLS)
        out = (jnp.zeros((CONV_ROWS, CONV_COLS), _F32) if after is None
               else _zeros_after(after[:CONV_ROWS, :CONV_COLS]))
        for r in range(SUBLANES):
            part = None
            for m in range(2 * HALO // SUBLANES):
                k = SUBLANES * m + r - 1
                if 0 <= k < CONV_K:
                    lo = r0 + SUBLANES * m
                    term = buf_ref[lo:lo + CONV_ROWS + SUBLANES, cols] * cw_ref[k:k + 1, cols]
                    part = term if part is None else part + term
            out = out + part[r:r + CONV_ROWS]
        c_ref[r0:r0 + CONV_ROWS, cols] = out

    return [functools.partial(piece, r0, c0)
            for r0 in range(0, tm, CONV_ROWS) for c0 in range(0, CONV_WIDTH, CONV_COLS)]


def _mem_kv_kernel(mem_ref, g_ref, wkt_ref, wv_ref, kt_ref, v_ref):
    m = _rmsnorm(mem_ref[...], g_ref[...]).astype(_BF16)
    kt = lax.dot_general(wkt_ref[...], m, (((1,), (1,)), ((), ())), preferred_element_type=_F32)
    kt_ref[...] = kt.astype(_BF16)
    v_ref[...] = _dot(m, wv_ref[...]).astype(_BF16)


def _mem_kv(mem, g, wkt, wv):
    bsz = mem.shape[0]
    return pl.pallas_call(
        _mem_kv_kernel,
        grid=(bsz,),
        in_specs=[pl.BlockSpec((None, N_MEM, D_MODEL), lambda b: (b, 0, 0)),
                  _const_spec((1, D_MODEL)), _const_spec(wkt.shape), _const_spec(wv.shape)],
        out_specs=[pl.BlockSpec((None, D_MODEL, N_MEM), lambda b: (b, 0, 0)),
                   pl.BlockSpec((None, N_MEM, D_MODEL), lambda b: (b, 0, 0))],
        out_shape=[jax.ShapeDtypeStruct((bsz, D_MODEL, N_MEM), _BF16),
                   jax.ShapeDtypeStruct((bsz, N_MEM, D_MODEL), _BF16)],
        compiler_params=pltpu.CompilerParams(dimension_semantics=("arbitrary",)),
        name="mem_kv",
    )(mem, g, wkt, wv)


def _post_kernel(v_ref, vnext_ref, vafter_ref, zs_ref, ys_ref, gate_ref, h1_ref, kt_ref, vm_ref,
                 cw_ref, cb_ref, lng_ref, lnb_ref, wpw_ref, d_ref, wglu_ref, wout_ref,
                 gx_ref, wq_ref, wo_ref, g2_ref, wgu_ref, wd_ref, gf_ref,
                 out_ref, buf_ref, c_ref, act_ref, o_ref, *, tm):
    i = pl.program_id(1)
    last = pl.num_programs(1) - 1
    no_rows = jnp.zeros((HALO, CONV_WIDTH), _F32)

    @pl.when(i == 0)
    def _first_tile():
        _stage_conv_input(buf_ref, no_rows, v_ref[...],
                          jnp.where(last > 0, vnext_ref[0:HALO, :], no_rows), tm)
        for piece in _depthwise_conv_pieces(buf_ref, cw_ref, c_ref, tm):
            piece()

    c = c_ref[...] + cb_ref[...]
    c = c - jnp.mean(c, axis=-1, keepdims=True)
    c = c * lax.rsqrt(jnp.mean(c * c, axis=-1, keepdims=True) + EPS) * lng_ref[...] + lnb_ref[...]
    conv_out = _dot((c * jax.nn.sigmoid(c)).astype(_BF16), wpw_ref[...])

    y = ys_ref[...] + d_ref[...] * zs_ref[...]
    ag = _dot(jax.nn.gelu(y).astype(_BF16), wglu_ref[...])
    ssm_out = ag[:, :D_MODEL] * jax.nn.sigmoid(ag[:, D_MODEL:])
    gate = gate_ref[...]
    merged = gate[:, :D_MODEL] * conv_out + gate[:, D_MODEL:] * ssm_out
    h = h1_ref[...] + _dot(merged.astype(_BF16), wout_ref[...])

    q = _dot(_rmsnorm(h, gx_ref[...]).astype(_BF16), wq_ref[...]).astype(_BF16)
    for hd in range(X_HEADS):
        sl = slice(hd * X_HEAD_DIM, (hd + 1) * X_HEAD_DIM)
        s = _dot(q[:, sl], kt_ref[sl, :]) * (X_HEAD_DIM ** -0.5)
        e = jnp.exp(s - jnp.max(s, axis=-1, keepdims=True))
        p = e / jnp.sum(e, axis=-1, keepdims=True)
        o_ref[:, sl] = _dot(p.astype(_BF16), vm_ref[:, sl]).astype(_BF16)
    h = h + _dot(o_ref[...], wo_ref[...])
    xn = _rmsnorm(h, g2_ref[...]).astype(_BF16)
    _stage_conv_input(buf_ref, v_ref[tm - HALO:tm, :], vnext_ref[...],
                      jnp.where(i + 1 < last, vafter_ref[...], no_rows), tm)
    h = h + 0.5 * _swiglu(xn, wgu_ref, wd_ref, act_ref,
                          side_work=_depthwise_conv_pieces(buf_ref, cw_ref, c_ref, tm))
    out_ref[...] = _rmsnorm(h, gf_ref[...])


def _post(v, zs, ys, gate, h1, kt, vm, p):
    bsz, length, _ = v.shape
    tm = POST_ROW_TILE
    n_tiles = length // tm
    per = tm // HALO
    n_halo = length // HALO
    seq = lambda w: pl.BlockSpec((None, tm, w), lambda b, i: (b, i, 0))
    nxt = pl.BlockSpec((None, tm, CONV_WIDTH), lambda b, i: (b, jnp.minimum(i + 1, n_tiles - 1), 0))
    after = pl.BlockSpec((None, HALO, CONV_WIDTH),
                         lambda b, i: (b, jnp.minimum((i + 2) * per, n_halo - 1), 0))
    consts = [p['conv_w'], p['conv_b'], p['conv_ln_g'], p['conv_ln_b'], p['conv_w_pw'], p['ssm_d'],
              p['ssm_w_glu'], p['w_out'], p['xattn_g'], p['xattn_wq'], p['xattn_wo'], p['ffn2_g'],
              p['ffn2_wgu'], p['ffn2_wd'], p['final_g']]
    return pl.pallas_call(
        functools.partial(_post_kernel, tm=tm),
        grid=(bsz, n_tiles),
        in_specs=[seq(CONV_WIDTH), nxt, after, seq(SSM_WIDTH), seq(SSM_WIDTH), seq(2 * D_MODEL),
                  seq(D_MODEL),
                  pl.BlockSpec((None, D_MODEL, N_MEM), lambda b, i: (b, 0, 0)),
                  pl.BlockSpec((None, N_MEM, D_MODEL), lambda b, i: (b, 0, 0))]
                 + [_const_spec(c.shape) for c in consts],
        out_specs=seq(D_MODEL),
        out_shape=jax.ShapeDtypeStruct((bsz, length, D_MODEL), _F32),
        scratch_shapes=[pltpu.VMEM((tm + 2 * HALO, CONV_WIDTH), _F32),
                        pltpu.VMEM((tm, CONV_WIDTH), _F32),
                        pltpu.VMEM((tm, D_FF), _BF16),
                        pltpu.VMEM((tm, D_MODEL), _BF16)],
        compiler_params=pltpu.CompilerParams(dimension_semantics=("arbitrary", "arbitrary"),
                                             vmem_limit_bytes=VMEM_LIMIT),
        name="post",
    )(v, v, v, zs, ys, gate, h1, kt, vm, *consts)


def _row(vec):
    return vec.reshape(1, -1).astype(_F32)


def _trunk(x, mem, p):
    bsz, length, _ = x.shape
    h1, v, zs, gate = _ffn_inproj(x.reshape(bsz * length, D_MODEL), p['ffn1_g'], p['ffn1_wgu'],
                                  p['ffn1_wd'], p['mix_g'], p['w_in'], p['b_in'])
    seq = lambda a: a.reshape(bsz, length, a.shape[-1])
    h1, v, zs, gate = seq(h1), seq(v), seq(zs), seq(gate)
    ys = _ssm_scan(zs, p['ssm_w'], p['ssm_mv'], p['ssm_coef'])
    kt, vm = _mem_kv(mem, p['mem_g'], p['xattn_wkt'], p['xattn_wv'])
    return _post(v, zs, ys, gate, h1, kt, vm, p)


def kernel(x_prompt, x_sample, mem_prompt, mem_sample, ffn1_g, ffn1_wgu, ffn1_wd, mix_g, w_in, b_in, conv_w, conv_b, conv_ln_g, conv_ln_b, conv_w_pw, ssm_lam_re, ssm_lam_im, ssm_log_dt, ssm_b_re, ssm_b_im, ssm_c_re, ssm_c_im, ssm_d, ssm_w_glu, w_out, xattn_g, mem_g, xattn_wq, xattn_wkv, xattn_wo, ffn2_g, ffn2_wgu, ffn2_wd, final_g):
    assert ffn1_g.shape[0] == 1, "single-layer trunk"
    p = {}
    p['ffn1_g'], p['mix_g'], p['xattn_g'] = _row(ffn1_g[0]), _row(mix_g[0]), _row(xattn_g[0])
    p['mem_g'], p['ffn2_g'], p['final_g'] = _row(mem_g[0]), _row(ffn2_g[0]), _row(final_g)
    p['ffn1_wgu'], p['ffn2_wgu'] = ffn1_wgu[0].astype(_BF16), ffn2_wgu[0].astype(_BF16)
    p['ffn1_wd'], p['ffn2_wd'] = ffn1_wd[0].astype(_BF16), ffn2_wd[0].astype(_BF16)
    p['w_in'], p['b_in'] = w_in[0].astype(_BF16), _row(b_in[0])
    p['conv_w'], p['conv_b'] = conv_w[0].astype(_F32), _row(conv_b[0])
    p['conv_ln_g'], p['conv_ln_b'] = _row(conv_ln_g[0]), _row(conv_ln_b[0])
    p['conv_w_pw'] = conv_w_pw[0].astype(_BF16)
    p['ssm_w'], p['ssm_mv'], p['ssm_coef'] = _ssm_tables(
        ssm_lam_re[0], ssm_lam_im[0], ssm_log_dt[0], ssm_b_re[0], ssm_b_im[0],
        ssm_c_re[0], ssm_c_im[0])
    p['ssm_d'], p['ssm_w_glu'] = _row(ssm_d[0]), ssm_w_glu[0].astype(_BF16)
    p['w_out'] = w_out[0].astype(_BF16)
    p['xattn_wq'], p['xattn_wo'] = xattn_wq[0].astype(_BF16), xattn_wo[0].astype(_BF16)
    p['xattn_wkt'] = xattn_wkv[0][:, :D_MODEL].T.astype(_BF16)
    p['xattn_wv'] = xattn_wkv[0][:, D_MODEL:].astype(_BF16)
    return (_trunk(x_prompt, mem_prompt, p), _trunk(x_sample, mem_sample, p))
```

```python
import functools

import jax
import jax.numpy as jnp
import numpy as np
from jax import lax
from jax.experimental import pallas as pl
from jax.experimental.pallas import tpu as pltpu

D_MODEL = 1024
D_FF = 2816
FF_CHUNK = 256
N_FF_CHUNKS = D_FF // FF_CHUNK
CONV_WIDTH = 512
CONV_K = 31
CONV_PAD = CONV_K // 2
SUBLANES = 8
LANES = 128
HALO = 2 * SUBLANES
CONV_ROWS = 64
CONV_COLS = 256
SSM_WIDTH = 512
SSM_GROUP = 16
SSM_GROUPS = SSM_WIDTH // SSM_GROUP
SSM_STATE = 64
CHUNK = 16
FLAT = CHUNK * SSM_GROUP
GROUPS_PER_BLOCK = LANES // SSM_GROUP
RELAYOUT_ROWS_IN = 64
RELAYOUT_ROWS_OUT = 32
SCAN_STEPS = 8
N_MEM = 256
X_HEADS = 4
X_HEAD_DIM = D_MODEL // X_HEADS
EPS = 1e-6
IN_COLS = 2 * CONV_WIDTH + SSM_WIDTH + 2 * D_MODEL

ROW_TILE = 512
FFN_ROW_TILE = 256
VMEM_LIMIT = 56 * 1024 * 1024
SCAN_VMEM_BUDGET = 44 * 1024 * 1024

_F32 = jnp.float32
_BF16 = jnp.bfloat16


def _dot(a, b):
    return jnp.dot(a, b, preferred_element_type=_F32)


def _rmsnorm(x, g):
    return x * lax.rsqrt(jnp.mean(x * x, axis=-1, keepdims=True) + EPS) * g


def _const_spec(shape):
    nd = len(shape)
    return pl.BlockSpec(shape, lambda *_: (0,) * nd, pipeline_mode=pl.Buffered(1))


def _swiglu(xn, wgu_ref, wd_ref, act_ref):
    for j in range(N_FF_CHUNKS):
        lo = j * FF_CHUNK
        g = _dot(xn, wgu_ref[:, lo:lo + FF_CHUNK])
        u = _dot(xn, wgu_ref[:, D_FF + lo:D_FF + lo + FF_CHUNK])
        act_ref[:, lo:lo + FF_CHUNK] = (g * jax.nn.sigmoid(g) * u).astype(_BF16)
    return _dot(act_ref[...], wd_ref[...])


def _ffn_inproj_kernel(x_ref, g1_ref, wgu_ref, wd_ref, gmix_ref, win_ref, bin_ref,
                       h1_ref, v_ref, zs_ref, gate_ref, act_ref):
    x = x_ref[...]
    xn = _rmsnorm(x, g1_ref[...]).astype(_BF16)
    h1 = x + 0.5 * _swiglu(xn, wgu_ref, wd_ref, act_ref)
    h1_ref[...] = h1
    un = _rmsnorm(h1, gmix_ref[...]).astype(_BF16)
    z = _dot(un, win_ref[...]) + bin_ref[...]
    v_ref[...] = z[:, :CONV_WIDTH] * jax.nn.sigmoid(z[:, CONV_WIDTH:2 * CONV_WIDTH])
    zs_ref[...] = z[:, 2 * CONV_WIDTH:2 * CONV_WIDTH + SSM_WIDTH]
    gate_ref[...] = jax.nn.sigmoid(z[:, 2 * CONV_WIDTH + SSM_WIDTH:])


def _ffn_inproj(x, g1, wgu, wd, gmix, win, b_in):
    rows = x.shape[0]
    tm = FFN_ROW_TILE
    row = lambda w: pl.BlockSpec((tm, w), lambda i: (i, 0))
    return pl.pallas_call(
        _ffn_inproj_kernel,
        grid=(rows // tm,),
        in_specs=[row(D_MODEL), _const_spec((1, D_MODEL)),
                  _const_spec(wgu.shape), _const_spec(wd.shape),
                  _const_spec((1, D_MODEL)), _const_spec(win.shape), _const_spec((1, IN_COLS))],
        out_specs=[row(D_MODEL), row(CONV_WIDTH), row(SSM_WIDTH), row(2 * D_MODEL)],
        out_shape=[jax.ShapeDtypeStruct((rows, D_MODEL), _F32),
                   jax.ShapeDtypeStruct((rows, CONV_WIDTH), _F32),
                   jax.ShapeDtypeStruct((rows, SSM_WIDTH), _F32),
                   jax.ShapeDtypeStruct((rows, 2 * D_MODEL), _F32)],
        scratch_shapes=[pltpu.VMEM((tm, D_FF), _BF16)],
        compiler_params=pltpu.CompilerParams(dimension_semantics=("arbitrary",),
                                             vmem_limit_bytes=VMEM_LIMIT),
        name="ffn_inproj",
    )(x, g1, wgu, wd, gmix, win, b_in)


def _piece_masks(shape):
    piece = lax.broadcasted_iota(jnp.int32, shape, 1) // SSM_GROUP
    return [piece == p for p in range(1, GROUPS_PER_BLOCK)]


def _diagonal_select(xs, masks):
    nb = GROUPS_PER_BLOCK
    out = []
    for k in range(nb):
        acc = xs[-k % nb]
        for p in range(1, nb):
            acc = jnp.where(masks[p - 1], xs[(p - k) % nb], acc)
        out.append(acc)
    return out


def _chunk_time_order():
    g = np.arange(SSM_GROUPS)[:, None] % GROUPS_PER_BLOCK
    q = np.arange(CHUNK)[None, :]
    nb = GROUPS_PER_BLOCK
    return (q // nb) * nb + (q % nb - g) % nb


def _ssm_scan_kernel(z_ref, w_ref, mv_ref, coef_ref, y_ref, a_ref, sf_ref, sb_ref, yy_ref,
                     *, n_chunks):
    nb = GROUPS_PER_BLOCK
    halves = CHUNK // nb

    def relayout_in(i, carry):
        rr = RELAYOUT_ROWS_IN
        row0 = pl.multiple_of(i * (rr * CHUNK), rr * CHUNK)
        crow = pl.multiple_of(i * rr, rr)
        masks = _piece_masks((rr // 2, LANES))
        for half in range(halves):
            xs = [pltpu.bitcast(z_ref[pl.ds(row0 + half * nb + t, rr, stride=CHUNK), :].astype(_BF16),
                                jnp.uint32) for t in range(nb)]
            xs = [x if t == 0 else pltpu.roll(x, SSM_GROUP * t, 1) for t, x in enumerate(xs)]
            for g, x in enumerate(_diagonal_select(xs, masks)):
                a_ref[g, pl.ds(crow, rr), half * LANES:(half + 1) * LANES] = pltpu.bitcast(x, _BF16)
        return carry

    lax.fori_loop(0, n_chunks // RELAYOUT_ROWS_IN, relayout_in, 0)

    for g in range(nb):
        s = _dot(a_ref[g], w_ref[g])
        packed = pl.ds(g, n_chunks, stride=nb)
        for d, s_ref in enumerate((sf_ref, sb_ref)):
            own = s[:, d * LANES:(d + 1) * LANES]
            s_ref[packed, :] = own
            yy_ref[d, packed, :] = pltpu.roll(own, SSM_STATE, 1)

    def scan(d, s_ref, a, b, bp, reverse):
        n_blocks = n_chunks // SCAN_STEPS
        span = SCAN_STEPS * nb

        def rows(blk):
            return pl.ds(pl.multiple_of(blk * span, span), span)

        def load(blk):
            return s_ref[rows(blk), :], yy_ref[d, rows(blk), :]

        def body(i, carry):
            v, w, s_all, sw_all = carry
            blk = n_blocks - 1 - i if reverse else i
            ahead = jnp.maximum(blk - 1, 0) if reverse else jnp.minimum(blk + 1, n_blocks - 1)
            s_next, sw_next = load(ahead)
            entering = [None] * SCAN_STEPS
            for k in (range(SCAN_STEPS - 1, -1, -1) if reverse else range(SCAN_STEPS)):
                entering[k] = v
                own = slice(k * nb, (k + 1) * nb)
                v, w = a * v + b * w + s_all[own], a * w + bp * v + sw_all[own]
            s_ref[rows(blk), :] = jnp.concatenate(entering, axis=0)
            return v, w, s_next, sw_next

        zero = jnp.zeros((nb, LANES), _F32)
        lax.fori_loop(0, n_blocks, body, (zero, zero) + load(n_blocks - 1 if reverse else 0))

    scan(0, sf_ref, coef_ref[0], coef_ref[1], coef_ref[2], reverse=False)
    scan(1, sb_ref, coef_ref[3], coef_ref[4], coef_ref[5], reverse=True)

    for g in range(nb):
        lhs = jnp.concatenate([a_ref[g],
                               sf_ref[pl.ds(g, n_chunks, stride=nb), :].astype(_BF16),
                               sb_ref[pl.ds(g, n_chunks, stride=nb), :].astype(_BF16)], axis=1)
        yg = _dot(lhs, mv_ref[g])
        for half in range(halves):
            yy_ref[half, g * n_chunks:(g + 1) * n_chunks, :] = yg[:, half * LANES:(half + 1) * LANES]

    def relayout_out(i, carry):
        rr = RELAYOUT_ROWS_OUT
        row0 = pl.multiple_of(i * (rr * CHUNK), rr * CHUNK)
        crow = pl.multiple_of(i * rr, rr)
        masks = _piece_masks((rr, LANES))
        for half in range(halves):
            ys = [yy_ref[half, pl.ds(g * n_chunks + crow, rr), :] for g in range(nb)]
            for t, y in enumerate(_diagonal_select(ys, masks)):
                if t:
                    y = pltpu.roll(y, LANES - SSM_GROUP * t, 1)
                y_ref[pl.ds(row0 + half * nb + t, rr, stride=CHUNK), :] = y
        return carry

    lax.fori_loop(0, n_chunks // RELAYOUT_ROWS_OUT, relayout_out, 0)


def _ssm_scan(zs, w, mv, coef):
    bsz, length, _ = zs.shape
    n_chunks = length // CHUNK
    nb = GROUPS_PER_BLOCK
    block_bytes = length * LANES * 4
    scratch_bytes = n_chunks * nb * (FLAT * 2 + 2 * LANES * 4 + FLAT * 4)
    table_bytes = nb * (FLAT * FLAT + 2 * FLAT * FLAT) * 2
    double = 4 * block_bytes + scratch_bytes + table_bytes <= SCAN_VMEM_BUDGET
    seq_mode = pl.Buffered(2 if double else 1)
    once = pl.Buffered(1)
    seq_spec = pl.BlockSpec((None, length, LANES), lambda j, b: (b, 0, j), pipeline_mode=seq_mode)
    return pl.pallas_call(
        functools.partial(_ssm_scan_kernel, n_chunks=n_chunks),
        grid=(SSM_GROUPS // nb, bsz),
        in_specs=[seq_spec,
                  pl.BlockSpec((nb, FLAT, FLAT), lambda j, b: (j, 0, 0), pipeline_mode=once),
                  pl.BlockSpec((nb, 2 * FLAT, FLAT), lambda j, b: (j, 0, 0), pipeline_mode=once),
                  pl.BlockSpec((None, 6, nb, LANES), lambda j, b: (j, 0, 0, 0), pipeline_mode=once)],
        out_specs=seq_spec,
        out_shape=jax.ShapeDtypeStruct(zs.shape, _F32),
        scratch_shapes=[pltpu.VMEM((nb, n_chunks, FLAT), _BF16),
                        pltpu.VMEM((n_chunks * nb, LANES), _F32),
                        pltpu.VMEM((n_chunks * nb, LANES), _F32),
                        pltpu.VMEM((2, n_chunks * nb, LANES), _F32)],
        compiler_params=pltpu.CompilerParams(dimension_semantics=("arbitrary", "arbitrary"),
                                             vmem_limit_bytes=VMEM_LIMIT),
        name="ssm_scan",
    )(zs, w, mv, coef)


def _ssm_direction_tables(lam_re, lam_im, log_dt, b_re, b_im, c_re, c_im):
    dt = jnp.exp(log_dt)[:, None]
    mag = jnp.exp(lam_re * dt)
    lr, li = mag * jnp.cos(lam_im * dt), mag * jnp.sin(lam_im * dt)
    den = lam_re * lam_re + lam_im * lam_im
    qr = ((lr - 1.0) * lam_re + li * lam_im) / den
    qi = (li * lam_re - (lr - 1.0) * lam_im) / den
    bb_re = qr[..., None] * b_re - qi[..., None] * b_im
    bb_im = qr[..., None] * b_im + qi[..., None] * b_re
    p_re, p_im = [jnp.ones_like(lr)], [jnp.zeros_like(lr)]
    for _ in range(CHUNK):
        p_re, p_im = (p_re + [p_re[-1] * lr - p_im[-1] * li],
                      p_im + [p_re[-1] * li + p_im[-1] * lr])
    p_re, p_im = jnp.stack(p_re), jnp.stack(p_im)
    cp_re = c_re[None] * p_re[:, :, None, :] - c_im[None] * p_im[:, :, None, :]
    cp_im = c_re[None] * p_im[:, :, None, :] + c_im[None] * p_re[:, :, None, :]
    kern = (jnp.einsum('kgpn,gnh->kgph', cp_re[:CHUNK], bb_re, precision=lax.Precision.HIGHEST)
            - jnp.einsum('kgpn,gnh->kgph', cp_im[:CHUNK], bb_im, precision=lax.Precision.HIGHEST))
    return kern, p_re, p_im, bb_re, bb_im, cp_re, cp_im


def _take_power(table, power):
    idx = jnp.asarray(power.T).reshape((CHUNK, SSM_GROUPS) + (1,) * (table.ndim - 2))
    return jnp.moveaxis(jnp.take_along_axis(table, idx, axis=0), 0, 1)


def _ssm_tables(lam_re, lam_im, log_dt, b_re, b_im, c_re, c_im):
    G, N = SSM_GROUPS, SSM_STATE
    order = _chunk_time_order()
    lag = order[:, None, :] - order[:, :, None]
    toeplitz, w_parts, v_parts, coefs = 0.0, [], [], []
    for d in range(2):
        kern, p_re, p_im, bb_re, bb_im, cp_re, cp_im = _ssm_direction_tables(
            lam_re[d], lam_im[d], log_dt[d], b_re[d], b_im[d], c_re[d], c_im[d])
        rel = lag if d == 0 else -lag
        idx = jnp.asarray(np.clip(rel, 0, CHUNK - 1).reshape(G, CHUNK * CHUNK, 1, 1))
        toe = jnp.take_along_axis(kern.transpose(1, 0, 2, 3), idx, axis=1)
        toe = jnp.where(jnp.asarray(rel.reshape(G, CHUNK * CHUNK, 1, 1) >= 0), toe, 0.0)
        toe = toe.reshape(G, CHUNK, CHUNK, SSM_GROUP, SSM_GROUP)
        toeplitz = toeplitz + toe.transpose(0, 1, 4, 2, 3).reshape(G, FLAT, FLAT)
        pw = CHUNK - 1 - order if d == 0 else order
        pw_re, pw_im = _take_power(p_re, pw), _take_power(p_im, pw)
        w_re = pw_re[..., None] * bb_re[:, None] - pw_im[..., None] * bb_im[:, None]
        w_im = pw_re[..., None] * bb_im[:, None] + pw_im[..., None] * bb_re[:, None]
        w_parts += [w_re.transpose(0, 1, 3, 2).reshape(G, FLAT, N),
                    w_im.transpose(0, 1, 3, 2).reshape(G, FLAT, N)]
        po = order + 1 if d == 0 else CHUNK - order
        v_parts += [_take_power(cp_re, po).transpose(0, 3, 1, 2).reshape(G, N, FLAT),
                    -_take_power(cp_im, po).transpose(0, 3, 1, 2).reshape(G, N, FLAT)]
        qr, qi = p_re[CHUNK], p_im[CHUNK]
        coefs += [jnp.concatenate([qr, qr], -1), jnp.concatenate([-qi, qi], -1),
                  jnp.concatenate([qi, -qi], -1)]
    w = jnp.concatenate(w_parts, axis=-1)
    mv = jnp.concatenate([toeplitz] + v_parts, axis=1)
    coef = jnp.stack(coefs).reshape(6, G // GROUPS_PER_BLOCK, GROUPS_PER_BLOCK, 2 * N)
    return w.astype(_BF16), mv.astype(_BF16), coef.transpose(1, 0, 2, 3)


def _depthwise_conv(buf_ref, cw_ref, c_ref, tm):
    def rows(i, carry):
        r0 = pl.multiple_of(i * CONV_ROWS, CONV_ROWS)
        for c0 in range(0, CONV_WIDTH, CONV_COLS):
            cols = slice(c0, c0 + CONV_COLS)
            out = jnp.zeros((CONV_ROWS, CONV_COLS), _F32)
            for r in range(SUBLANES):
                part = None
                for m in range(HALO * 2 // SUBLANES):
                    k = SUBLANES * m + r - 1
                    if 0 <= k < CONV_K:
                        term = (buf_ref[pl.ds(r0 + SUBLANES * m, CONV_ROWS + SUBLANES), cols]
                                * cw_ref[k:k + 1, cols])
                        part = term if part is None else part + term
                out = out + part[r:r + CONV_ROWS]
            c_ref[pl.ds(r0, CONV_ROWS), cols] = out
        return carry

    lax.fori_loop(0, tm // CONV_ROWS, rows, 0)


def _mix_kernel(v_ref, vp_ref, vn_ref, zs_ref, ys_ref, gate_ref, h1_ref,
                cw_ref, cb_ref, lng_ref, lnb_ref, wpw_ref, d_ref, wglu_ref, wout_ref,
                h2_ref, buf_ref, c_ref, *, tm):
    i = pl.program_id(1)
    last = pl.num_programs(1) - 1
    buf_ref[0:HALO, :] = jnp.where(i > 0, vp_ref[...], 0.0)
    buf_ref[HALO:HALO + tm, :] = v_ref[...]
    buf_ref[HALO + tm:2 * HALO + tm, :] = jnp.where(i < last, vn_ref[...], 0.0)
    _depthwise_conv(buf_ref, cw_ref, c_ref, tm)
    c = c_ref[...] + cb_ref[...]
    c = c - jnp.mean(c, axis=-1, keepdims=True)
    c = c * lax.rsqrt(jnp.mean(c * c, axis=-1, keepdims=True) + EPS) * lng_ref[...] + lnb_ref[...]
    conv_out = _dot((c * jax.nn.sigmoid(c)).astype(_BF16), wpw_ref[...])

    y = ys_ref[...] + d_ref[...] * zs_ref[...]
    ag = _dot(jax.nn.gelu(y).astype(_BF16), wglu_ref[...])
    ssm_out = ag[:, :D_MODEL] * jax.nn.sigmoid(ag[:, D_MODEL:])

    gate = gate_ref[...]
    merged = gate[:, :D_MODEL] * conv_out + gate[:, D_MODEL:] * ssm_out
    h2_ref[...] = h1_ref[...] + _dot(merged.astype(_BF16), wout_ref[...])


def _mix(v, zs, ys, gate, h1, cw, cb, lng, lnb, wpw, d, wglu, wout):
    bsz, length, _ = v.shape
    tm = ROW_TILE
    per = tm // HALO
    n_halo = length // HALO
    seq = lambda w: pl.BlockSpec((None, tm, w), lambda b, i: (b, i, 0))
    prev = pl.BlockSpec((None, HALO, CONV_WIDTH), lambda b, i: (b, jnp.maximum(i * per - 1, 0), 0))
    nxt = pl.BlockSpec((None, HALO, CONV_WIDTH),
                       lambda b, i: (b, jnp.minimum((i + 1) * per, n_halo - 1), 0))
    return pl.pallas_call(
        functools.partial(_mix_kernel, tm=tm),
        grid=(bsz, length // tm),
        in_specs=[seq(CONV_WIDTH), prev, nxt, seq(SSM_WIDTH), seq(SSM_WIDTH), seq(2 * D_MODEL),
                  seq(D_MODEL),
                  _const_spec(cw.shape), _const_spec((1, CONV_WIDTH)), _const_spec((1, CONV_WIDTH)),
                  _const_spec((1, CONV_WIDTH)), _const_spec(wpw.shape), _const_spec((1, SSM_WIDTH)),
                  _const_spec(wglu.shape), _const_spec(wout.shape)],
        out_specs=seq(D_MODEL),
        out_shape=jax.ShapeDtypeStruct((bsz, length, D_MODEL), _F32),
        scratch_shapes=[pltpu.VMEM((tm + 2 * HALO, CONV_WIDTH), _F32),
                        pltpu.VMEM((tm, CONV_WIDTH), _F32)],
        compiler_params=pltpu.CompilerParams(dimension_semantics=("arbitrary", "arbitrary"),
                                             vmem_limit_bytes=VMEM_LIMIT),
        name="mix",
    )(v, v, v, zs, ys, gate, h1, cw, cb, lng, lnb, wpw, d, wglu, wout)


def _mem_kv_kernel(mem_ref, g_ref, wkt_ref, wv_ref, kt_ref, v_ref):
    m = _rmsnorm(mem_ref[...], g_ref[...]).astype(_BF16)
    kt = lax.dot_general(wkt_ref[...], m, (((1,), (1,)), ((), ())), preferred_element_type=_F32)
    kt_ref[...] = kt.astype(_BF16)
    v_ref[...] = _dot(m, wv_ref[...]).astype(_BF16)


def _mem_kv(mem, g, wkt, wv):
    bsz = mem.shape[0]
    return pl.pallas_call(
        _mem_kv_kernel,
        grid=(bsz,),
        in_specs=[pl.BlockSpec((None, N_MEM, D_MODEL), lambda b: (b, 0, 0)),
                  _const_spec((1, D_MODEL)), _const_spec(wkt.shape), _const_spec(wv.shape)],
        out_specs=[pl.BlockSpec((None, D_MODEL, N_MEM), lambda b: (b, 0, 0)),
                   pl.BlockSpec((None, N_MEM, D_MODEL), lambda b: (b, 0, 0))],
        out_shape=[jax.ShapeDtypeStruct((bsz, D_MODEL, N_MEM), _BF16),
                   jax.ShapeDtypeStruct((bsz, N_MEM, D_MODEL), _BF16)],
        compiler_params=pltpu.CompilerParams(dimension_semantics=("arbitrary",)),
        name="mem_kv",
    )(mem, g, wkt, wv)


def _attn_ffn_kernel(h_ref, kt_ref, v_ref, gx_ref, wq_ref, wo_ref, g2_ref, wgu_ref, wd_ref,
                     gf_ref, out_ref, act_ref, o_ref):
    h = h_ref[...]
    q = _dot(_rmsnorm(h, gx_ref[...]).astype(_BF16), wq_ref[...]).astype(_BF16)
    for hd in range(X_HEADS):
        sl = slice(hd * X_HEAD_DIM, (hd + 1) * X_HEAD_DIM)
        s = _dot(q[:, sl], kt_ref[sl, :]) * (X_HEAD_DIM ** -0.5)
        e = jnp.exp(s - jnp.max(s, axis=-1, keepdims=True))
        p = e / jnp.sum(e, axis=-1, keepdims=True)
        o_ref[:, sl] = _dot(p.astype(_BF16), v_ref[:, sl]).astype(_BF16)
    h = h + _dot(o_ref[...], wo_ref[...])
    xn = _rmsnorm(h, g2_ref[...]).astype(_BF16)
    h = h + 0.5 * _swiglu(xn, wgu_ref, wd_ref, act_ref)
    out_ref[...] = _rmsnorm(h, gf_ref[...])


def _attn_ffn(h, kt, v, gx, wq, wo, g2, wgu, wd, gf):
    bsz, length, _ = h.shape
    tm = ROW_TILE
    seq = pl.BlockSpec((None, tm, D_MODEL), lambda b, i: (b, i, 0))
    return pl.pallas_call(
        _attn_ffn_kernel,
        grid=(bsz, length // tm),
        in_specs=[seq,
                  pl.BlockSpec((None, D_MODEL, N_MEM), lambda b, i: (b, 0, 0)),
                  pl.BlockSpec((None, N_MEM, D_MODEL), lambda b, i: (b, 0, 0)),
                  _const_spec((1, D_MODEL)), _const_spec(wq.shape), _const_spec(wo.shape),
                  _const_spec((1, D_MODEL)), _const_spec(wgu.shape), _const_spec(wd.shape),
                  _const_spec((1, D_MODEL))],
        out_specs=seq,
        out_shape=jax.ShapeDtypeStruct(h.shape, _F32),
        scratch_shapes=[pltpu.VMEM((tm, D_FF), _BF16), pltpu.VMEM((tm, D_MODEL), _BF16)],
        compiler_params=pltpu.CompilerParams(dimension_semantics=("arbitrary", "arbitrary"),
                                             vmem_limit_bytes=VMEM_LIMIT),
        name="attn_ffn",
    )(h, kt, v, gx, wq, wo, g2, wgu, wd, gf)


def _row(vec):
    return vec.reshape(1, -1).astype(_F32)


def _trunk(x, mem, p):
    bsz, length, _ = x.shape
    h1, v, zs, gate = _ffn_inproj(x.reshape(bsz * length, D_MODEL), p['ffn1_g'], p['ffn1_wgu'],
                                  p['ffn1_wd'], p['mix_g'], p['w_in'], p['b_in'])
    seq = lambda a: a.reshape(bsz, length, a.shape[-1])
    h1, v, zs, gate = seq(h1), seq(v), seq(zs), seq(gate)
    ys = _ssm_scan(zs, p['ssm_w'], p['ssm_mv'], p['ssm_coef'])
    h2 = _mix(v, zs, ys, gate, h1, p['conv_w'], p['conv_b'], p['conv_ln_g'], p['conv_ln_b'],
              p['conv_w_pw'], p['ssm_d'], p['ssm_w_glu'], p['w_out'])
    kt, vm = _mem_kv(mem, p['mem_g'], p['xattn_wkt'], p['xattn_wv'])
    return _attn_ffn(h2, kt, vm, p['xattn_g'], p['xattn_wq'], p['xattn_wo'], p['ffn2_g'],
                     p['ffn2_wgu'], p['ffn2_wd'], p['final_g'])


def kernel(x_prompt, x_sample, mem_prompt, mem_sample, ffn1_g, ffn1_wgu, ffn1_wd, mix_g, w_in, b_in, conv_w, conv_b, conv_ln_g, conv_ln_b, conv_w_pw, ssm_lam_re, ssm_lam_im, ssm_log_dt, ssm_b_re, ssm_b_im, ssm_c_re, ssm_c_im, ssm_d, ssm_w_glu, w_out, xattn_g, mem_g, xattn_wq, xattn_wkv, xattn_wo, ffn2_g, ffn2_wgu, ffn2_wd, final_g):
    assert ffn1_g.shape[0] == 1, "single-layer trunk"
    p = {}
    p['ffn1_g'], p['mix_g'], p['xattn_g'] = _row(ffn1_g[0]), _row(mix_g[0]), _row(xattn_g[0])
    p['mem_g'], p['ffn2_g'], p['final_g'] = _row(mem_g[0]), _row(ffn2_g[0]), _row(final_g)
    p['ffn1_wgu'], p['ffn2_wgu'] = ffn1_wgu[0].astype(_BF16), ffn2_wgu[0].astype(_BF16)
    p['ffn1_wd'], p['ffn2_wd'] = ffn1_wd[0].astype(_BF16), ffn2_wd[0].astype(_BF16)
    p['w_in'], p['b_in'] = w_in[0].astype(_BF16), _row(b_in[0])
    p['conv_w'], p['conv_b'] = conv_w[0].astype(_F32), _row(conv_b[0])
    p['conv_ln_g'], p['conv_ln_b'] = _row(conv_ln_g[0]), _row(conv_ln_b[0])
    p['conv_w_pw'] = conv_w_pw[0].astype(_BF16)
    p['ssm_w'], p['ssm_mv'], p['ssm_coef'] = _ssm_tables(
        ssm_lam_re[0], ssm_lam_im[0], ssm_log_dt[0], ssm_b_re[0], ssm_b_im[0],
        ssm_c_re[0], ssm_c_im[0])
    p['ssm_d'], p['ssm_w_glu'] = _row(ssm_d[0]), ssm_w_glu[0].astype(_BF16)
    p['w_out'] = w_out[0].astype(_BF16)
    p['xattn_wq'], p['xattn_wo'] = xattn_wq[0].astype(_BF16), xattn_wo[0].astype(_BF16)
    p['xattn_wkt'] = xattn_wkv[0][:, :D_MODEL].T.astype(_BF16)
    p['xattn_wv'] = xattn_wkv[0][:, D_MODEL:].astype(_BF16)
    return (_trunk(x_prompt, mem_prompt, p), _trunk(x_sample, mem_sample, p))
```

```python
import functools

import jax
import jax.numpy as jnp
from jax import lax
from jax.experimental import pallas as pl
from jax.experimental.pallas import tpu as pltpu

D_MODEL = 1024
D_FF = 2816
FF_CHUNK = 256
N_FF_CHUNKS = D_FF // FF_CHUNK
CONV_WIDTH = 512
CONV_K = 31
CONV_PAD = CONV_K // 2
SUBLANES = 8
LANES = 128
HALO = 2 * SUBLANES
CONV_ROWS = 64
CONV_COLS = 256
SSM_WIDTH = 512
SSM_GROUP = 16
SSM_GROUPS = SSM_WIDTH // SSM_GROUP
SSM_STATE = 64
CHUNK = 16
FLAT = CHUNK * SSM_GROUP
GROUPS_PER_BLOCK = LANES // SSM_GROUP
RELAYOUT_ROWS_IN = 64
RELAYOUT_ROWS_OUT = 32
SCAN_STEPS = 8
N_MEM = 256
X_HEADS = 4
X_HEAD_DIM = D_MODEL // X_HEADS
EPS = 1e-6
IN_COLS = 2 * CONV_WIDTH + SSM_WIDTH + 2 * D_MODEL

ROW_TILE = 512
FFN_ROW_TILE = 256
VMEM_LIMIT = 56 * 1024 * 1024
SCAN_VMEM_BUDGET = 44 * 1024 * 1024

_F32 = jnp.float32
_BF16 = jnp.bfloat16


def _dot(a, b):
    return jnp.dot(a, b, preferred_element_type=_F32)


def _rmsnorm(x, g):
    return x * lax.rsqrt(jnp.mean(x * x, axis=-1, keepdims=True) + EPS) * g


def _const_spec(shape):
    nd = len(shape)
    return pl.BlockSpec(shape, lambda *_: (0,) * nd, pipeline_mode=pl.Buffered(1))


def _swiglu(xn, wgu_ref, wd_ref, act_ref):
    for j in range(N_FF_CHUNKS):
        lo = j * FF_CHUNK
        g = _dot(xn, wgu_ref[:, lo:lo + FF_CHUNK])
        u = _dot(xn, wgu_ref[:, D_FF + lo:D_FF + lo + FF_CHUNK])
        act_ref[:, lo:lo + FF_CHUNK] = (g * jax.nn.sigmoid(g) * u).astype(_BF16)
    return _dot(act_ref[...], wd_ref[...])


def _ffn_inproj_kernel(x_ref, g1_ref, wgu_ref, wd_ref, gmix_ref, win_ref, bin_ref,
                       h1_ref, v_ref, zs_ref, gate_ref, act_ref):
    x = x_ref[...]
    xn = _rmsnorm(x, g1_ref[...]).astype(_BF16)
    h1 = x + 0.5 * _swiglu(xn, wgu_ref, wd_ref, act_ref)
    h1_ref[...] = h1
    un = _rmsnorm(h1, gmix_ref[...]).astype(_BF16)
    z = _dot(un, win_ref[...]) + bin_ref[...]
    v_ref[...] = z[:, :CONV_WIDTH] * jax.nn.sigmoid(z[:, CONV_WIDTH:2 * CONV_WIDTH])
    zs_ref[...] = z[:, 2 * CONV_WIDTH:2 * CONV_WIDTH + SSM_WIDTH]
    gate_ref[...] = jax.nn.sigmoid(z[:, 2 * CONV_WIDTH + SSM_WIDTH:])


def _ffn_inproj(x, g1, wgu, wd, gmix, win, b_in):
    rows = x.shape[0]
    tm = FFN_ROW_TILE
    row = lambda w: pl.BlockSpec((tm, w), lambda i: (i, 0))
    return pl.pallas_call(
        _ffn_inproj_kernel,
        grid=(rows // tm,),
        in_specs=[row(D_MODEL), _const_spec((1, D_MODEL)),
                  _const_spec(wgu.shape), _const_spec(wd.shape),
                  _const_spec((1, D_MODEL)), _const_spec(win.shape), _const_spec((1, IN_COLS))],
        out_specs=[row(D_MODEL), row(CONV_WIDTH), row(SSM_WIDTH), row(2 * D_MODEL)],
        out_shape=[jax.ShapeDtypeStruct((rows, D_MODEL), _F32),
                   jax.ShapeDtypeStruct((rows, CONV_WIDTH), _F32),
                   jax.ShapeDtypeStruct((rows, SSM_WIDTH), _F32),
                   jax.ShapeDtypeStruct((rows, 2 * D_MODEL), _F32)],
        scratch_shapes=[pltpu.VMEM((tm, D_FF), _BF16)],
        compiler_params=pltpu.CompilerParams(dimension_semantics=("arbitrary",),
                                             vmem_limit_bytes=VMEM_LIMIT),
        name="ffn_inproj",
    )(x, g1, wgu, wd, gmix, win, b_in)


def _piece_masks(shape):
    piece = lax.broadcasted_iota(jnp.int32, shape, 1) // SSM_GROUP
    return [piece == p for p in range(1, GROUPS_PER_BLOCK)]


def _diagonal_select(xs, masks):
    nb = GROUPS_PER_BLOCK
    out = []
    for k in range(nb):
        acc = xs[-k % nb]
        for p in range(1, nb):
            acc = jnp.where(masks[p - 1], xs[(p - k) % nb], acc)
        out.append(acc)
    return out


def _ssm_scan_kernel(z_ref, w_ref, mv_ref, coef_ref, y_ref, a_ref, sf_ref, sb_ref, yy_ref,
                     *, n_chunks):
    nb = GROUPS_PER_BLOCK
    halves = CHUNK // nb

    def relayout_in(i, carry):
        rr = RELAYOUT_ROWS_IN
        row0 = pl.multiple_of(i * (rr * CHUNK), rr * CHUNK)
        crow = pl.multiple_of(i * rr, rr)
        masks = _piece_masks((rr // 2, LANES))
        for half in range(halves):
            xs = [pltpu.bitcast(z_ref[pl.ds(row0 + half * nb + t, rr, stride=CHUNK), :].astype(_BF16),
                                jnp.uint32) for t in range(nb)]
            xs = [x if t == 0 else pltpu.roll(x, SSM_GROUP * t, 1) for t, x in enumerate(xs)]
            for g, x in enumerate(_diagonal_select(xs, masks)):
                a_ref[g, pl.ds(crow, rr), half * LANES:(half + 1) * LANES] = pltpu.bitcast(x, _BF16)
        return carry

    lax.fori_loop(0, n_chunks // RELAYOUT_ROWS_IN, relayout_in, 0)

    for g in range(nb):
        s = _dot(a_ref[g], w_ref[g])
        packed = pl.ds(g, n_chunks, stride=nb)
        for d, s_ref in enumerate((sf_ref, sb_ref)):
            own = s[:, d * LANES:(d + 1) * LANES]
            s_ref[packed, :] = own
            yy_ref[d, packed, :] = pltpu.roll(own, SSM_STATE, 1)

    n_blocks = n_chunks // SCAN_STEPS
    span = SCAN_STEPS * nb
    s_refs = (sf_ref, sb_ref)
    coefs = [(coef_ref[3 * d], coef_ref[3 * d + 1], coef_ref[3 * d + 2]) for d in range(2)]

    def rows(blk):
        return pl.ds(pl.multiple_of(blk * span, span), span)

    def load(d, blk):
        return s_refs[d][rows(blk), :], yy_ref[d, rows(blk), :]

    def advance(d, blk, state):
        v, w, s_all, sw_all = state
        a, b, bp = coefs[d]
        entering = [None] * SCAN_STEPS
        for k in (range(SCAN_STEPS) if d == 0 else range(SCAN_STEPS - 1, -1, -1)):
            entering[k] = v
            own = slice(k * nb, (k + 1) * nb)
            v, w = a * v + b * w + s_all[own], a * w + bp * v + sw_all[own]
        s_refs[d][rows(blk), :] = jnp.concatenate(entering, axis=0)
        return v, w

    def scan_body(i, carry):
        fwd, bwd = carry
        blk_f, blk_b = i, n_blocks - 1 - i
        next_f = load(0, jnp.minimum(blk_f + 1, n_blocks - 1))
        next_b = load(1, jnp.maximum(blk_b - 1, 0))
        return advance(0, blk_f, fwd) + next_f, advance(1, blk_b, bwd) + next_b

    zero = jnp.zeros((nb, LANES), _F32)
    lax.fori_loop(0, n_blocks, scan_body,
                  ((zero, zero) + load(0, 0), (zero, zero) + load(1, n_blocks - 1)))

    for g in range(nb):
        lhs = jnp.concatenate([a_ref[g],
                               sf_ref[pl.ds(g, n_chunks, stride=nb), :].astype(_BF16),
                               sb_ref[pl.ds(g, n_chunks, stride=nb), :].astype(_BF16)], axis=1)
        yg = _dot(lhs, mv_ref[g])
        for half in range(halves):
            yy_ref[half, g * n_chunks:(g + 1) * n_chunks, :] = yg[:, half * LANES:(half + 1) * LANES]

    def relayout_out(i, carry):
        rr = RELAYOUT_ROWS_OUT
        row0 = pl.multiple_of(i * (rr * CHUNK), rr * CHUNK)
        crow = pl.multiple_of(i * rr, rr)
        masks = _piece_masks((rr, LANES))
        for half in range(halves):
            ys = [yy_ref[half, pl.ds(g * n_chunks + crow, rr), :] for g in range(nb)]
            for t, y in enumerate(_diagonal_select(ys, masks)):
                if t:
                    y = pltpu.roll(y, LANES - SSM_GROUP * t, 1)
                y_ref[pl.ds(row0 + half * nb + t, rr, stride=CHUNK), :] = y
        return carry

    lax.fori_loop(0, n_chunks // RELAYOUT_ROWS_OUT, relayout_out, 0)


def _ssm_scan(zs, w, mv, coef):
    bsz, length, _ = zs.shape
    n_chunks = length // CHUNK
    nb = GROUPS_PER_BLOCK
    block_bytes = length * LANES * 4
    scratch_bytes = n_chunks * nb * (FLAT * 2 + 2 * LANES * 4 + FLAT * 4)
    table_bytes = nb * (FLAT * FLAT + 2 * FLAT * FLAT) * 2
    double = 4 * block_bytes + scratch_bytes + table_bytes <= SCAN_VMEM_BUDGET
    seq_mode = pl.Buffered(2 if double else 1)
    once = pl.Buffered(1)
    seq_spec = pl.BlockSpec((None, length, LANES), lambda j, b: (b, 0, j), pipeline_mode=seq_mode)
    return pl.pallas_call(
        functools.partial(_ssm_scan_kernel, n_chunks=n_chunks),
        grid=(SSM_GROUPS // nb, bsz),
        in_specs=[seq_spec,
                  pl.BlockSpec((nb, FLAT, FLAT), lambda j, b: (j, 0, 0), pipeline_mode=once),
                  pl.BlockSpec((nb, 2 * FLAT, FLAT), lambda j, b: (j, 0, 0), pipeline_mode=once),
                  pl.BlockSpec((None, 6, nb, LANES), lambda j, b: (j, 0, 0, 0), pipeline_mode=once)],
        out_specs=seq_spec,
        out_shape=jax.ShapeDtypeStruct(zs.shape, _F32),
        scratch_shapes=[pltpu.VMEM((nb, n_chunks, FLAT), _BF16),
                        pltpu.VMEM((n_chunks * nb, LANES), _F32),
                        pltpu.VMEM((n_chunks * nb, LANES), _F32),
                        pltpu.VMEM((2, n_chunks * nb, LANES), _F32)],
        compiler_params=pltpu.CompilerParams(dimension_semantics=("arbitrary", "arbitrary"),
                                             vmem_limit_bytes=VMEM_LIMIT),
        name="ssm_scan",
    )(zs, w, mv, coef)


def _ssm_direction_tables(lam_re, lam_im, log_dt, b_re, b_im, c_re, c_im):
    dt = jnp.exp(log_dt)[:, None]
    mag = jnp.exp(lam_re * dt)
    lr, li = mag * jnp.cos(lam_im * dt), mag * jnp.sin(lam_im * dt)
    den = lam_re * lam_re + lam_im * lam_im
    qr = ((lr - 1.0) * lam_re + li * lam_im) / den
    qi = (li * lam_re - (lr - 1.0) * lam_im) / den
    bb_re = qr[..., None] * b_re - qi[..., None] * b_im
    bb_im = qr[..., None] * b_im + qi[..., None] * b_re
    p_re, p_im = [jnp.ones_like(lr)], [jnp.zeros_like(lr)]
    for _ in range(CHUNK):
        p_re, p_im = (p_re + [p_re[-1] * lr - p_im[-1] * li],
                      p_im + [p_re[-1] * li + p_im[-1] * lr])
    p_re, p_im = jnp.stack(p_re), jnp.stack(p_im)
    cp_re = c_re[None] * p_re[:, :, None, :] - c_im[None] * p_im[:, :, None, :]
    cp_im = c_re[None] * p_im[:, :, None, :] + c_im[None] * p_re[:, :, None, :]
    kern = (jnp.einsum('kgpn,gnh->kgph', cp_re[:CHUNK], bb_re, precision=lax.Precision.HIGHEST)
            - jnp.einsum('kgpn,gnh->kgph', cp_im[:CHUNK], bb_im, precision=lax.Precision.HIGHEST))
    return kern, p_re, p_im, bb_re, bb_im, cp_re, cp_im


def _rotate_time(x, axis):
    nb = GROUPS_PER_BLOCK
    shape = x.shape
    x = x.reshape((SSM_GROUPS // nb, nb) + shape[1:axis] + (CHUNK // nb, nb) + shape[axis + 1:])
    parts = [jnp.roll(x[:, r], r, axis=axis + 1) for r in range(nb)]
    return jnp.stack(parts, axis=1).reshape(shape)


def _toeplitz(kern, forward):
    kg = kern.transpose(1, 0, 2, 3)
    zeros = jnp.zeros_like(kg[:, :CHUNK - 1])
    band = jnp.concatenate([zeros, kg] if forward else [kg[:, ::-1], zeros], axis=1)
    return jnp.stack([band[:, CHUNK - 1 - s:2 * CHUNK - 1 - s] for s in range(CHUNK)], axis=1)


def _ssm_tables(lam_re, lam_im, log_dt, b_re, b_im, c_re, c_im):
    G, N = SSM_GROUPS, SSM_STATE
    toeplitz, w_parts, v_parts, coefs = 0.0, [], [], []
    for d in range(2):
        kern, p_re, p_im, bb_re, bb_im, cp_re, cp_im = _ssm_direction_tables(
            lam_re[d], lam_im[d], log_dt[d], b_re[d], b_im[d], c_re[d], c_im[d])
        toeplitz = toeplitz + _toeplitz(kern, forward=d == 0)
        pw_re, pw_im = p_re[:CHUNK], p_im[:CHUNK]
        if d == 0:
            pw_re, pw_im = pw_re[::-1], pw_im[::-1]
        w_re = pw_re[..., None] * bb_re[None] - pw_im[..., None] * bb_im[None]
        w_im = pw_re[..., None] * bb_im[None] + pw_im[..., None] * bb_re[None]
        w_parts += [_rotate_time(w.transpose(1, 0, 3, 2), 1).reshape(G, FLAT, N)
                    for w in (w_re, w_im)]
        out_re, out_im = cp_re[1:], -cp_im[1:]
        if d == 1:
            out_re, out_im = out_re[::-1], out_im[::-1]
        v_parts += [_rotate_time(v.transpose(1, 3, 0, 2), 2).reshape(G, N, FLAT)
                    for v in (out_re, out_im)]
        qr, qi = p_re[CHUNK], p_im[CHUNK]
        coefs += [jnp.concatenate([qr, qr], -1), jnp.concatenate([-qi, qi], -1),
                  jnp.concatenate([qi, -qi], -1)]
    toeplitz = _rotate_time(_rotate_time(toeplitz.transpose(0, 1, 4, 2, 3), 1), 3)
    w = jnp.concatenate(w_parts, axis=-1)
    mv = jnp.concatenate([toeplitz.reshape(G, FLAT, FLAT)] + v_parts, axis=1)
    coef = jnp.stack(coefs).reshape(6, G // GROUPS_PER_BLOCK, GROUPS_PER_BLOCK, 2 * N)
    return w.astype(_BF16), mv.astype(_BF16), coef.transpose(1, 0, 2, 3)


def _depthwise_conv(buf_ref, cw_ref, c_ref, tm):
    n_tiles = CONV_ROWS // SUBLANES
    sublane = lax.broadcasted_iota(jnp.int32, (SUBLANES, CONV_COLS), 0)

    def rows(i, carry):
        r0 = pl.multiple_of(i * CONV_ROWS, CONV_ROWS)
        for c0 in range(0, CONV_WIDTH, CONV_COLS):
            cols = slice(c0, c0 + CONV_COLS)
            out = [None] * n_tiles
            for r in range(SUBLANES):
                part = None
                for m in range(HALO * 2 // SUBLANES):
                    k = SUBLANES * m + r - 1
                    if 0 <= k < CONV_K:
                        term = (buf_ref[pl.ds(r0 + SUBLANES * m, CONV_ROWS + SUBLANES), cols]
                                * cw_ref[k:k + 1, cols])
                        part = term if part is None else part + term
                tiles = [part[SUBLANES * j:SUBLANES * (j + 1)] for j in range(n_tiles + 1)]
                for j in range(n_tiles):
                    if r == 0:
                        shifted = tiles[j]
                    else:
                        shifted = pltpu.roll(jnp.where(sublane < r, tiles[j + 1], tiles[j]),
                                             SUBLANES - r, 0)
                    out[j] = shifted if out[j] is None else out[j] + shifted
            c_ref[pl.ds(r0, CONV_ROWS), cols] = jnp.concatenate(out, axis=0)
        return carry

    lax.fori_loop(0, tm // CONV_ROWS, rows, 0)


def _mix_kernel(v_ref, vp_ref, vn_ref, zs_ref, ys_ref, gate_ref, h1_ref,
                cw_ref, cb_ref, lng_ref, lnb_ref, wpw_ref, d_ref, wglu_ref, wout_ref,
                h2_ref, buf_ref, c_ref, *, tm):
    i = pl.program_id(1)
    last = pl.num_programs(1) - 1
    buf_ref[0:HALO, :] = jnp.where(i > 0, vp_ref[...], 0.0)
    buf_ref[HALO:HALO + tm, :] = v_ref[...]
    buf_ref[HALO + tm:2 * HALO + tm, :] = jnp.where(i < last, vn_ref[...], 0.0)
    _depthwise_conv(buf_ref, cw_ref, c_ref, tm)
    c = c_ref[...] + cb_ref[...]
    c = c - jnp.mean(c, axis=-1, keepdims=True)
    c = c * lax.rsqrt(jnp.mean(c * c, axis=-1, keepdims=True) + EPS) * lng_ref[...] + lnb_ref[...]
    conv_out = _dot((c * jax.nn.sigmoid(c)).astype(_BF16), wpw_ref[...])

    y = ys_ref[...] + d_ref[...] * zs_ref[...]
    ag = _dot(jax.nn.gelu(y).astype(_BF16), wglu_ref[...])
    ssm_out = ag[:, :D_MODEL] * jax.nn.sigmoid(ag[:, D_MODEL:])

    gate = gate_ref[...]
    merged = gate[:, :D_MODEL] * conv_out + gate[:, D_MODEL:] * ssm_out
    h2_ref[...] = h1_ref[...] + _dot(merged.astype(_BF16), wout_ref[...])


def _mix(v, zs, ys, gate, h1, cw, cb, lng, lnb, wpw, d, wglu, wout):
    bsz, length, _ = v.shape
    tm = ROW_TILE
    per = tm // HALO
    n_halo = length // HALO
    seq = lambda w: pl.BlockSpec((None, tm, w), lambda b, i: (b, i, 0))
    prev = pl.BlockSpec((None, HALO, CONV_WIDTH), lambda b, i: (b, jnp.maximum(i * per - 1, 0), 0))
    nxt = pl.BlockSpec((None, HALO, CONV_WIDTH),
                       lambda b, i: (b, jnp.minimum((i + 1) * per, n_halo - 1), 0))
    return pl.pallas_call(
        functools.partial(_mix_kernel, tm=tm),
        grid=(bsz, length // tm),
        in_specs=[seq(CONV_WIDTH), prev, nxt, seq(SSM_WIDTH), seq(SSM_WIDTH), seq(2 * D_MODEL),
                  seq(D_MODEL),
                  _const_spec(cw.shape), _const_spec((1, CONV_WIDTH)), _const_spec((1, CONV_WIDTH)),
                  _const_spec((1, CONV_WIDTH)), _const_spec(wpw.shape), _const_spec((1, SSM_WIDTH)),
                  _const_spec(wglu.shape), _const_spec(wout.shape)],
        out_specs=seq(D_MODEL),
        out_shape=jax.ShapeDtypeStruct((bsz, length, D_MODEL), _F32),
        scratch_shapes=[pltpu.VMEM((tm + 2 * HALO, CONV_WIDTH), _F32),
                        pltpu.VMEM((tm, CONV_WIDTH), _F32)],
        compiler_params=pltpu.CompilerParams(dimension_semantics=("arbitrary", "arbitrary"),
                                             vmem_limit_bytes=VMEM_LIMIT),
        name="mix",
    )(v, v, v, zs, ys, gate, h1, cw, cb, lng, lnb, wpw, d, wglu, wout)


def _mem_kv_kernel(mem_ref, g_ref, wkt_ref, wv_ref, kt_ref, v_ref):
    m = _rmsnorm(mem_ref[...], g_ref[...]).astype(_BF16)
    kt = lax.dot_general(wkt_ref[...], m, (((1,), (1,)), ((), ())), preferred_element_type=_F32)
    kt_ref[...] = kt.astype(_BF16)
    v_ref[...] = _dot(m, wv_ref[...]).astype(_BF16)


def _mem_kv(mem, g, wkt, wv):
    bsz = mem.shape[0]
    return pl.pallas_call(
        _mem_kv_kernel,
        grid=(bsz,),
        in_specs=[pl.BlockSpec((None, N_MEM, D_MODEL), lambda b: (b, 0, 0)),
                  _const_spec((1, D_MODEL)), _const_spec(wkt.shape), _const_spec(wv.shape)],
        out_specs=[pl.BlockSpec((None, D_MODEL, N_MEM), lambda b: (b, 0, 0)),
                   pl.BlockSpec((None, N_MEM, D_MODEL), lambda b: (b, 0, 0))],
        out_shape=[jax.ShapeDtypeStruct((bsz, D_MODEL, N_MEM), _BF16),
                   jax.ShapeDtypeStruct((bsz, N_MEM, D_MODEL), _BF16)],
        compiler_params=pltpu.CompilerParams(dimension_semantics=("arbitrary",)),
        name="mem_kv",
    )(mem, g, wkt, wv)


def _attn_ffn_kernel(h_ref, kt_ref, v_ref, gx_ref, wq_ref, wo_ref, g2_ref, wgu_ref, wd_ref,
                     gf_ref, out_ref, act_ref, o_ref):
    h = h_ref[...]
    q = _dot(_rmsnorm(h, gx_ref[...]).astype(_BF16), wq_ref[...]).astype(_BF16)
    for hd in range(X_HEADS):
        sl = slice(hd * X_HEAD_DIM, (hd + 1) * X_HEAD_DIM)
        s = _dot(q[:, sl], kt_ref[sl, :]) * (X_HEAD_DIM ** -0.5)
        e = jnp.exp(s - jnp.max(s, axis=-1, keepdims=True))
        p = e / jnp.sum(e, axis=-1, keepdims=True)
        o_ref[:, sl] = _dot(p.astype(_BF16), v_ref[:, sl]).astype(_BF16)
    h = h + _dot(o_ref[...], wo_ref[...])
    xn = _rmsnorm(h, g2_ref[...]).astype(_BF16)
    h = h + 0.5 * _swiglu(xn, wgu_ref, wd_ref, act_ref)
    out_ref[...] = _rmsnorm(h, gf_ref[...])


def _attn_ffn(h, kt, v, gx, wq, wo, g2, wgu, wd, gf):
    bsz, length, _ = h.shape
    tm = ROW_TILE
    seq = pl.BlockSpec((None, tm, D_MODEL), lambda b, i: (b, i, 0))
    return pl.pallas_call(
        _attn_ffn_kernel,
        grid=(bsz, length // tm),
        in_specs=[seq,
                  pl.BlockSpec((None, D_MODEL, N_MEM), lambda b, i: (b, 0, 0)),
                  pl.BlockSpec((None, N_MEM, D_MODEL), lambda b, i: (b, 0, 0)),
                  _const_spec((1, D_MODEL)), _const_spec(wq.shape), _const_spec(wo.shape),
                  _const_spec((1, D_MODEL)), _const_spec(wgu.shape), _const_spec(wd.shape),
                  _const_spec((1, D_MODEL))],
        out_specs=seq,
        out_shape=jax.ShapeDtypeStruct(h.shape, _F32),
        scratch_shapes=[pltpu.VMEM((tm, D_FF), _BF16), pltpu.VMEM((tm, D_MODEL), _BF16)],
        compiler_params=pltpu.CompilerParams(dimension_semantics=("arbitrary", "arbitrary"),
                                             vmem_limit_bytes=VMEM_LIMIT),
        name="attn_ffn",
    )(h, kt, v, gx, wq, wo, g2, wgu, wd, gf)


def _row(vec):
    return vec.reshape(1, -1).astype(_F32)


def _trunk(x, mem, p):
    bsz, length, _ = x.shape
    h1, v, zs, gate = _ffn_inproj(x.reshape(bsz * length, D_MODEL), p['ffn1_g'], p['ffn1_wgu'],
                                  p['ffn1_wd'], p['mix_g'], p['w_in'], p['b_in'])
    seq = lambda a: a.reshape(bsz, length, a.shape[-1])
    h1, v, zs, gate = seq(h1), seq(v), seq(zs), seq(gate)
    ys = _ssm_scan(zs, p['ssm_w'], p['ssm_mv'], p['ssm_coef'])
    h2 = _mix(v, zs, ys, gate, h1, p['conv_w'], p['conv_b'], p['conv_ln_g'], p['conv_ln_b'],
              p['conv_w_pw'], p['ssm_d'], p['ssm_w_glu'], p['w_out'])
    kt, vm = _mem_kv(mem, p['mem_g'], p['xattn_wkt'], p['xattn_wv'])
    return _attn_ffn(h2, kt, vm, p['xattn_g'], p['xattn_wq'], p['xattn_wo'], p['ffn2_g'],
                     p['ffn2_wgu'], p['ffn2_wd'], p['final_g'])


def kernel(x_prompt, x_sample, mem_prompt, mem_sample, ffn1_g, ffn1_wgu, ffn1_wd, mix_g, w_in, b_in, conv_w, conv_b, conv_ln_g, conv_ln_b, conv_w_pw, ssm_lam_re, ssm_lam_im, ssm_log_dt, ssm_b_re, ssm_b_im, ssm_c_re, ssm_c_im, ssm_d, ssm_w_glu, w_out, xattn_g, mem_g, xattn_wq, xattn_wkv, xattn_wo, ffn2_g, ffn2_wgu, ffn2_wd, final_g):
    assert ffn1_g.shape[0] == 1, "single-layer trunk"
    p = {}
    p['ffn1_g'], p['mix_g'], p['xattn_g'] = _row(ffn1_g[0]), _row(mix_g[0]), _row(xattn_g[0])
    p['mem_g'], p['ffn2_g'], p['final_g'] = _row(mem_g[0]), _row(ffn2_g[0]), _row(final_g)
    p['ffn1_wgu'], p['ffn2_wgu'] = ffn1_wgu[0].astype(_BF16), ffn2_wgu[0].astype(_BF16)
    p['ffn1_wd'], p['ffn2_wd'] = ffn1_wd[0].astype(_BF16), ffn2_wd[0].astype(_BF16)
    p['w_in'], p['b_in'] = w_in[0].astype(_BF16), _row(b_in[0])
    p['conv_w'], p['conv_b'] = conv_w[0].astype(_F32), _row(conv_b[0])
    p['conv_ln_g'], p['conv_ln_b'] = _row(conv_ln_g[0]), _row(conv_ln_b[0])
    p['conv_w_pw'] = conv_w_pw[0].astype(_BF16)
    p['ssm_w'], p['ssm_mv'], p['ssm_coef'] = _ssm_tables(
        ssm_lam_re[0], ssm_lam_im[0], ssm_log_dt[0], ssm_b_re[0], ssm_b_im[0],
        ssm_c_re[0], ssm_c_im[0])
    p['ssm_d'], p['ssm_w_glu'] = _row(ssm_d[0]), ssm_w_glu[0].astype(_BF16)
    p['w_out'] = w_out[0].astype(_BF16)
    p['xattn_wq'], p['xattn_wo'] = xattn_wq[0].astype(_BF16), xattn_wo[0].astype(_BF16)
    p['xattn_wkt'] = xattn_wkv[0][:, :D_MODEL].T.astype(_BF16)
    p['xattn_wv'] = xattn_wkv[0][:, D_MODEL:].astype(_BF16)
    return (_trunk(x_prompt, mem_prompt, p), _trunk(x_sample, mem_sample, p))
```

```python
import functools

import jax
import jax.numpy as jnp
import numpy as np
from jax import lax
from jax.experimental import pallas as pl
from jax.experimental.pallas import tpu as pltpu

D_MODEL = 1024
D_FF = 2816
FF_CHUNK = 256
N_FF_CHUNKS = D_FF // FF_CHUNK
CONV_WIDTH = 512
CONV_K = 31
CONV_PAD = CONV_K // 2
SUBLANES = 8
LANES = 128
HALO = 2 * SUBLANES
CONV_ROWS = 64
CONV_COLS = 256
SSM_WIDTH = 512
SSM_GROUP = 16
SSM_GROUPS = SSM_WIDTH // SSM_GROUP
SSM_STATE = 64
CHUNK = 16
FLAT = CHUNK * SSM_GROUP
GROUPS_PER_BLOCK = LANES // SSM_GROUP
RELAYOUT_ROWS_IN = 64
RELAYOUT_ROWS_OUT = 32
SCAN_STEPS = 8
N_MEM = 256
X_HEADS = 4
X_HEAD_DIM = D_MODEL // X_HEADS
EPS = 1e-6
IN_COLS = 2 * CONV_WIDTH + SSM_WIDTH + 2 * D_MODEL

ROW_TILE = 512
FFN_ROW_TILE = 512
VMEM_LIMIT = 60 * 1024 * 1024
SCAN_VMEM_BUDGET = 44 * 1024 * 1024

_F32 = jnp.float32
_BF16 = jnp.bfloat16


def _dot(a, b):
    return jnp.dot(a, b, preferred_element_type=_F32)


def _rmsnorm(x, g):
    return x * lax.rsqrt(jnp.mean(x * x, axis=-1, keepdims=True) + EPS) * g


def _const_spec(shape):
    nd = len(shape)
    return pl.BlockSpec(shape, lambda *_: (0,) * nd, pipeline_mode=pl.Buffered(1))


def _swiglu(xn, wgu_ref, wd_ref, act_ref):
    for j in range(N_FF_CHUNKS):
        lo = j * FF_CHUNK
        g = _dot(xn, wgu_ref[:, lo:lo + FF_CHUNK])
        u = _dot(xn, wgu_ref[:, D_FF + lo:D_FF + lo + FF_CHUNK])
        act_ref[:, lo:lo + FF_CHUNK] = (g * jax.nn.sigmoid(g) * u).astype(_BF16)
    return _dot(act_ref[...], wd_ref[...])


def _ffn_inproj_kernel(x_ref, g1_ref, wgu_ref, wd_ref, gmix_ref, win_ref, bin_ref,
                       h1_ref, v_ref, zs_ref, gate_ref, act_ref):
    x = x_ref[...]
    xn = _rmsnorm(x, g1_ref[...]).astype(_BF16)
    h1 = x + 0.5 * _swiglu(xn, wgu_ref, wd_ref, act_ref)
    h1_ref[...] = h1
    un = _rmsnorm(h1, gmix_ref[...]).astype(_BF16)
    z = _dot(un, win_ref[...]) + bin_ref[...]
    v_ref[...] = z[:, :CONV_WIDTH] * jax.nn.sigmoid(z[:, CONV_WIDTH:2 * CONV_WIDTH])
    zs_ref[...] = z[:, 2 * CONV_WIDTH:2 * CONV_WIDTH + SSM_WIDTH]
    gate_ref[...] = jax.nn.sigmoid(z[:, 2 * CONV_WIDTH + SSM_WIDTH:])


def _ffn_inproj(x, g1, wgu, wd, gmix, win, b_in):
    rows = x.shape[0]
    tm = FFN_ROW_TILE
    row = lambda w: pl.BlockSpec((tm, w), lambda i: (i, 0))
    return pl.pallas_call(
        _ffn_inproj_kernel,
        grid=(rows // tm,),
        in_specs=[row(D_MODEL), _const_spec((1, D_MODEL)),
                  _const_spec(wgu.shape), _const_spec(wd.shape),
                  _const_spec((1, D_MODEL)), _const_spec(win.shape), _const_spec((1, IN_COLS))],
        out_specs=[row(D_MODEL), row(CONV_WIDTH), row(SSM_WIDTH), row(2 * D_MODEL)],
        out_shape=[jax.ShapeDtypeStruct((rows, D_MODEL), _F32),
                   jax.ShapeDtypeStruct((rows, CONV_WIDTH), _F32),
                   jax.ShapeDtypeStruct((rows, SSM_WIDTH), _F32),
                   jax.ShapeDtypeStruct((rows, 2 * D_MODEL), _F32)],
        scratch_shapes=[pltpu.VMEM((tm, D_FF), _BF16)],
        compiler_params=pltpu.CompilerParams(dimension_semantics=("arbitrary",),
                                             vmem_limit_bytes=VMEM_LIMIT),
        name="ffn_inproj",
    )(x, g1, wgu, wd, gmix, win, b_in)


def _piece_masks(shape):
    piece = lax.broadcasted_iota(jnp.int32, shape, 1) // SSM_GROUP
    return [piece == p for p in range(1, GROUPS_PER_BLOCK)]


def _diagonal_select(xs, masks):
    nb = GROUPS_PER_BLOCK
    out = []
    for k in range(nb):
        acc = xs[-k % nb]
        for p in range(1, nb):
            acc = jnp.where(masks[p - 1], xs[(p - k) % nb], acc)
        out.append(acc)
    return out


def _ssm_scan_kernel(z_ref, w_ref, mv_ref, coef_ref, y_ref, a_ref, sf_ref, sb_ref, yy_ref,
                     *, n_chunks):
    nb = GROUPS_PER_BLOCK
    halves = CHUNK // nb

    def relayout_in(i, carry):
        rr = RELAYOUT_ROWS_IN
        row0 = pl.multiple_of(i * (rr * CHUNK), rr * CHUNK)
        crow = pl.multiple_of(i * rr, rr)
        masks = _piece_masks((rr // 2, LANES))
        for half in range(halves):
            xs = [pltpu.bitcast(z_ref[pl.ds(row0 + half * nb + t, rr, stride=CHUNK), :].astype(_BF16),
                                jnp.uint32) for t in range(nb)]
            xs = [x if t == 0 else pltpu.roll(x, SSM_GROUP * t, 1) for t, x in enumerate(xs)]
            for g, x in enumerate(_diagonal_select(xs, masks)):
                a_ref[g, pl.ds(crow, rr), half * LANES:(half + 1) * LANES] = pltpu.bitcast(x, _BF16)
        return carry

    lax.fori_loop(0, n_chunks // RELAYOUT_ROWS_IN, relayout_in, 0)

    for g in range(nb):
        s = _dot(a_ref[g], w_ref[g])
        packed = pl.ds(g, n_chunks, stride=nb)
        for d, s_ref in enumerate((sf_ref, sb_ref)):
            own = s[:, d * LANES:(d + 1) * LANES]
            s_ref[packed, :] = own
            yy_ref[d, packed, :] = pltpu.roll(own, SSM_STATE, 1)

    n_blocks = n_chunks // SCAN_STEPS
    span = SCAN_STEPS * nb
    s_refs = (sf_ref, sb_ref)
    coefs = [(coef_ref[3 * d], coef_ref[3 * d + 1], coef_ref[3 * d + 2]) for d in range(2)]

    def rows(blk):
        return pl.ds(pl.multiple_of(blk * span, span), span)

    def load(d, blk):
        return s_refs[d][rows(blk), :], yy_ref[d, rows(blk), :]

    def advance(d, blk, state):
        v, w, s_all, sw_all = state
        a, b, bp = coefs[d]
        entering = [None] * SCAN_STEPS
        for k in (range(SCAN_STEPS) if d == 0 else range(SCAN_STEPS - 1, -1, -1)):
            entering[k] = v
            own = slice(k * nb, (k + 1) * nb)
            v, w = a * v + b * w + s_all[own], a * w + bp * v + sw_all[own]
        s_refs[d][rows(blk), :] = jnp.concatenate(entering, axis=0)
        return v, w

    def scan_body(i, carry):
        fwd, bwd = carry
        blk_f, blk_b = i, n_blocks - 1 - i
        next_f = load(0, jnp.minimum(blk_f + 1, n_blocks - 1))
        next_b = load(1, jnp.maximum(blk_b - 1, 0))
        return advance(0, blk_f, fwd) + next_f, advance(1, blk_b, bwd) + next_b

    zero = jnp.zeros((nb, LANES), _F32)
    lax.fori_loop(0, n_blocks, scan_body,
                  ((zero, zero) + load(0, 0), (zero, zero) + load(1, n_blocks - 1)))

    for g in range(nb):
        lhs = jnp.concatenate([a_ref[g],
                               sf_ref[pl.ds(g, n_chunks, stride=nb), :].astype(_BF16),
                               sb_ref[pl.ds(g, n_chunks, stride=nb), :].astype(_BF16)], axis=1)
        yg = _dot(lhs, mv_ref[g])
        for half in range(halves):
            yy_ref[half, g * n_chunks:(g + 1) * n_chunks, :] = yg[:, half * LANES:(half + 1) * LANES]

    def relayout_out(i, carry):
        rr = RELAYOUT_ROWS_OUT
        row0 = pl.multiple_of(i * (rr * CHUNK), rr * CHUNK)
        crow = pl.multiple_of(i * rr, rr)
        masks = _piece_masks((rr, LANES))
        for half in range(halves):
            ys = [yy_ref[half, pl.ds(g * n_chunks + crow, rr), :] for g in range(nb)]
            for t, y in enumerate(_diagonal_select(ys, masks)):
                if t:
                    y = pltpu.roll(y, LANES - SSM_GROUP * t, 1)
                y_ref[pl.ds(row0 + half * nb + t, rr, stride=CHUNK), :] = y
        return carry

    lax.fori_loop(0, n_chunks // RELAYOUT_ROWS_OUT, relayout_out, 0)


def _ssm_scan(zs, w, mv, coef):
    bsz, length, _ = zs.shape
    n_chunks = length // CHUNK
    nb = GROUPS_PER_BLOCK
    block_bytes = length * LANES * 4
    scratch_bytes = n_chunks * nb * (FLAT * 2 + 2 * LANES * 4 + FLAT * 4)
    table_bytes = nb * (FLAT * FLAT + 2 * FLAT * FLAT) * 2
    double = 4 * block_bytes + scratch_bytes + table_bytes <= SCAN_VMEM_BUDGET
    seq_mode = pl.Buffered(2 if double else 1)
    once = pl.Buffered(1)
    seq_spec = pl.BlockSpec((None, length, LANES), lambda j, b: (b, 0, j), pipeline_mode=seq_mode)
    return pl.pallas_call(
        functools.partial(_ssm_scan_kernel, n_chunks=n_chunks),
        grid=(SSM_GROUPS // nb, bsz),
        in_specs=[seq_spec,
                  pl.BlockSpec((nb, FLAT, FLAT), lambda j, b: (j, 0, 0), pipeline_mode=once),
                  pl.BlockSpec((nb, 2 * FLAT, FLAT), lambda j, b: (j, 0, 0), pipeline_mode=once),
                  pl.BlockSpec((None, 6, nb, LANES), lambda j, b: (j, 0, 0, 0), pipeline_mode=once)],
        out_specs=seq_spec,
        out_shape=jax.ShapeDtypeStruct(zs.shape, _F32),
        scratch_shapes=[pltpu.VMEM((nb, n_chunks, FLAT), _BF16),
                        pltpu.VMEM((n_chunks * nb, LANES), _F32),
                        pltpu.VMEM((n_chunks * nb, LANES), _F32),
                        pltpu.VMEM((2, n_chunks * nb, LANES), _F32)],
        compiler_params=pltpu.CompilerParams(dimension_semantics=("arbitrary", "arbitrary"),
                                             vmem_limit_bytes=VMEM_LIMIT),
        name="ssm_scan",
    )(zs, w, mv, coef)


def _table_selectors():
    nb = GROUPS_PER_BLOCK
    g = np.arange(SSM_GROUPS)[:, None] % nb
    q = np.arange(CHUNK)[None, :]
    order = (q // nb) * nb + (q % nb - g) % nb
    lag = order[:, None, :] - order[:, :, None]
    lags, powers = np.arange(CHUNK), np.arange(CHUNK + 1)
    sel_m = np.stack([lag[..., None] == lags, -lag[..., None] == lags], axis=3)
    sel_w = np.stack([(CHUNK - 1 - order)[..., None] == powers, order[..., None] == powers])
    sel_v = np.stack([(order + 1)[..., None] == powers, (CHUNK - order)[..., None] == powers])
    return [jnp.asarray(s, _F32) for s in (sel_m, sel_w, sel_v)]


def _ssm_tables(lam_re, lam_im, log_dt, b_re, b_im, c_re, c_im):
    G, N = SSM_GROUPS, SSM_STATE
    exact = lax.Precision.HIGHEST
    sel_m, sel_w, sel_v = _table_selectors()
    dt = jnp.exp(log_dt)[..., None]
    mag = jnp.exp(lam_re * dt)
    lr, li = mag * jnp.cos(lam_im * dt), mag * jnp.sin(lam_im * dt)
    p_re, p_im = lr[:, None], li[:, None]
    while p_re.shape[1] < CHUNK:
        top_re, top_im = p_re[:, -1:], p_im[:, -1:]
        p_re, p_im = (jnp.concatenate([p_re, p_re * top_re - p_im * top_im], axis=1),
                      jnp.concatenate([p_im, p_re * top_im + p_im * top_re], axis=1))
    p_re = jnp.concatenate([jnp.ones_like(lr)[:, None], p_re], axis=1)
    p_im = jnp.concatenate([jnp.zeros_like(li)[:, None], p_im], axis=1)
    den = lam_re * lam_re + lam_im * lam_im
    qr = ((lr - 1.0) * lam_re + li * lam_im) / den
    qi = (li * lam_re - (lr - 1.0) * lam_im) / den
    bb_re = qr[..., None] * b_re - qi[..., None] * b_im
    bb_im = qr[..., None] * b_im + qi[..., None] * b_re
    cp_re = c_re[:, None] * p_re[:, :, :, None, :] - c_im[:, None] * p_im[:, :, :, None, :]
    cp_im = c_re[:, None] * p_im[:, :, :, None, :] + c_im[:, None] * p_re[:, :, :, None, :]
    kern = (jnp.einsum('dkgon,dgni->dkgoi', cp_re[:, :CHUNK], bb_re, precision=exact)
            - jnp.einsum('dkgon,dgni->dkgoi', cp_im[:, :CHUNK], bb_im, precision=exact))
    toeplitz = jnp.einsum('gstdk,dkgoi->gsito', sel_m, kern, precision=exact).reshape(G, FLAT, FLAT)
    pb = jnp.stack([p_re[..., None] * bb_re[:, None] - p_im[..., None] * bb_im[:, None],
                    p_re[..., None] * bb_im[:, None] + p_im[..., None] * bb_re[:, None]], axis=1)
    w = jnp.einsum('dgqk,dckgni->gqidcn', sel_w, pb, precision=exact).reshape(G, FLAT, FLAT)
    cpv = jnp.stack([cp_re, -cp_im], axis=1)
    v = jnp.einsum('dgqk,dckgon->gdcnqo', sel_v, cpv, precision=exact).reshape(G, FLAT, FLAT)
    mv = jnp.concatenate([toeplitz, v], axis=1)
    fr, fi = p_re[:, CHUNK], p_im[:, CHUNK]
    coef = jnp.stack([jnp.concatenate([fr, fr], -1), jnp.concatenate([-fi, fi], -1),
                      jnp.concatenate([fi, -fi], -1)], axis=1)
    coef = coef.reshape(6, G // GROUPS_PER_BLOCK, GROUPS_PER_BLOCK, 2 * N).transpose(1, 0, 2, 3)
    return w.astype(_BF16), mv.astype(_BF16), coef


def _depthwise_conv(buf_ref, cw_ref, c_ref, tm):
    n_tiles = CONV_ROWS // SUBLANES
    sublane = lax.broadcasted_iota(jnp.int32, (SUBLANES, CONV_COLS), 0)

    def rows(i, carry):
        r0 = pl.multiple_of(i * CONV_ROWS, CONV_ROWS)
        for c0 in range(0, CONV_WIDTH, CONV_COLS):
            cols = slice(c0, c0 + CONV_COLS)
            out = [None] * n_tiles
            for r in range(SUBLANES):
                part = None
                for m in range(HALO * 2 // SUBLANES):
                    k = SUBLANES * m + r - 1
                    if 0 <= k < CONV_K:
                        term = (buf_ref[pl.ds(r0 + SUBLANES * m, CONV_ROWS + SUBLANES), cols]
                                * cw_ref[k:k + 1, cols])
                        part = term if part is None else part + term
                tiles = [part[SUBLANES * j:SUBLANES * (j + 1)] for j in range(n_tiles + 1)]
                for j in range(n_tiles):
                    if r == 0:
                        shifted = tiles[j]
                    else:
                        shifted = pltpu.roll(jnp.where(sublane < r, tiles[j + 1], tiles[j]),
                                             SUBLANES - r, 0)
                    out[j] = shifted if out[j] is None else out[j] + shifted
            c_ref[pl.ds(r0, CONV_ROWS), cols] = jnp.concatenate(out, axis=0)
        return carry

    lax.fori_loop(0, tm // CONV_ROWS, rows, 0)


def _mix_kernel(v_ref, vp_ref, vn_ref, zs_ref, ys_ref, gate_ref, h1_ref,
                cw_ref, cb_ref, lng_ref, lnb_ref, wpw_ref, d_ref, wglu_ref, wout_ref,
                h2_ref, buf_ref, c_ref, *, tm):
    i = pl.program_id(1)
    last = pl.num_programs(1) - 1
    buf_ref[0:HALO, :] = jnp.where(i > 0, vp_ref[...], 0.0)
    buf_ref[HALO:HALO + tm, :] = v_ref[...]
    buf_ref[HALO + tm:2 * HALO + tm, :] = jnp.where(i < last, vn_ref[...], 0.0)
    _depthwise_conv(buf_ref, cw_ref, c_ref, tm)
    c = c_ref[...] + cb_ref[...]
    c = c - jnp.mean(c, axis=-1, keepdims=True)
    c = c * lax.rsqrt(jnp.mean(c * c, axis=-1, keepdims=True) + EPS) * lng_ref[...] + lnb_ref[...]
    conv_out = _dot((c * jax.nn.sigmoid(c)).astype(_BF16), wpw_ref[...])

    y = ys_ref[...] + d_ref[...] * zs_ref[...]
    ag = _dot(jax.nn.gelu(y).astype(_BF16), wglu_ref[...])
    ssm_out = ag[:, :D_MODEL] * jax.nn.sigmoid(ag[:, D_MODEL:])

    gate = gate_ref[...]
    merged = gate[:, :D_MODEL] * conv_out + gate[:, D_MODEL:] * ssm_out
    h2_ref[...] = h1_ref[...] + _dot(merged.astype(_BF16), wout_ref[...])


def _mix(v, zs, ys, gate, h1, cw, cb, lng, lnb, wpw, d, wglu, wout):
    bsz, length, _ = v.shape
    tm = ROW_TILE
    per = tm // HALO
    n_halo = length // HALO
    seq = lambda w: pl.BlockSpec((None, tm, w), lambda b, i: (b, i, 0))
    prev = pl.BlockSpec((None, HALO, CONV_WIDTH), lambda b, i: (b, jnp.maximum(i * per - 1, 0), 0))
    nxt = pl.BlockSpec((None, HALO, CONV_WIDTH),
                       lambda b, i: (b, jnp.minimum((i + 1) * per, n_halo - 1), 0))
    return pl.pallas_call(
        functools.partial(_mix_kernel, tm=tm),
        grid=(bsz, length // tm),
        in_specs=[seq(CONV_WIDTH), prev, nxt, seq(SSM_WIDTH), seq(SSM_WIDTH), seq(2 * D_MODEL),
                  seq(D_MODEL),
                  _const_spec(cw.shape), _const_spec((1, CONV_WIDTH)), _const_spec((1, CONV_WIDTH)),
                  _const_spec((1, CONV_WIDTH)), _const_spec(wpw.shape), _const_spec((1, SSM_WIDTH)),
                  _const_spec(wglu.shape), _const_spec(wout.shape)],
        out_specs=seq(D_MODEL),
        out_shape=jax.ShapeDtypeStruct((bsz, length, D_MODEL), _F32),
        scratch_shapes=[pltpu.VMEM((tm + 2 * HALO, CONV_WIDTH), _F32),
                        pltpu.VMEM((tm, CONV_WIDTH), _F32)],
        compiler_params=pltpu.CompilerParams(dimension_semantics=("arbitrary", "arbitrary"),
                                             vmem_limit_bytes=VMEM_LIMIT),
        name="mix",
    )(v, v, v, zs, ys, gate, h1, cw, cb, lng, lnb, wpw, d, wglu, wout)


def _mem_kv_kernel(mem_ref, g_ref, wkt_ref, wv_ref, kt_ref, v_ref):
    m = _rmsnorm(mem_ref[...], g_ref[...]).astype(_BF16)
    kt = lax.dot_general(wkt_ref[...], m, (((1,), (1,)), ((), ())), preferred_element_type=_F32)
    kt_ref[...] = kt.astype(_BF16)
    v_ref[...] = _dot(m, wv_ref[...]).astype(_BF16)


def _mem_kv(mem, g, wkt, wv):
    bsz = mem.shape[0]
    return pl.pallas_call(
        _mem_kv_kernel,
        grid=(bsz,),
        in_specs=[pl.BlockSpec((None, N_MEM, D_MODEL), lambda b: (b, 0, 0)),
                  _const_spec((1, D_MODEL)), _const_spec(wkt.shape), _const_spec(wv.shape)],
        out_specs=[pl.BlockSpec((None, D_MODEL, N_MEM), lambda b: (b, 0, 0)),
                   pl.BlockSpec((None, N_MEM, D_MODEL), lambda b: (b, 0, 0))],
        out_shape=[jax.ShapeDtypeStruct((bsz, D_MODEL, N_MEM), _BF16),
                   jax.ShapeDtypeStruct((bsz, N_MEM, D_MODEL), _BF16)],
        compiler_params=pltpu.CompilerParams(dimension_semantics=("arbitrary",)),
        name="mem_kv",
    )(mem, g, wkt, wv)


def _attn_ffn_kernel(h_ref, kt_ref, v_ref, gx_ref, wq_ref, wo_ref, g2_ref, wgu_ref, wd_ref,
                     gf_ref, out_ref, act_ref, o_ref):
    h = h_ref[...]
    q = _dot(_rmsnorm(h, gx_ref[...]).astype(_BF16), wq_ref[...]).astype(_BF16)
    for hd in range(X_HEADS):
        sl = slice(hd * X_HEAD_DIM, (hd + 1) * X_HEAD_DIM)
        s = _dot(q[:, sl], kt_ref[sl, :]) * (X_HEAD_DIM ** -0.5)
        e = jnp.exp(s - jnp.max(s, axis=-1, keepdims=True))
        p = e / jnp.sum(e, axis=-1, keepdims=True)
        o_ref[:, sl] = _dot(p.astype(_BF16), v_ref[:, sl]).astype(_BF16)
    h = h + _dot(o_ref[...], wo_ref[...])
    xn = _rmsnorm(h, g2_ref[...]).astype(_BF16)
    h = h + 0.5 * _swiglu(xn, wgu_ref, wd_ref, act_ref)
    out_ref[...] = _rmsnorm(h, gf_ref[...])


def _attn_ffn(h, kt, v, gx, wq, wo, g2, wgu, wd, gf):
    bsz, length, _ = h.shape
    tm = ROW_TILE
    seq = pl.BlockSpec((None, tm, D_MODEL), lambda b, i: (b, i, 0))
    return pl.pallas_call(
        _attn_ffn_kernel,
        grid=(bsz, length // tm),
        in_specs=[seq,
                  pl.BlockSpec((None, D_MODEL, N_MEM), lambda b, i: (b, 0, 0)),
                  pl.BlockSpec((None, N_MEM, D_MODEL), lambda b, i: (b, 0, 0)),
                  _const_spec((1, D_MODEL)), _const_spec(wq.shape), _const_spec(wo.shape),
                  _const_spec((1, D_MODEL)), _const_spec(wgu.shape), _const_spec(wd.shape),
                  _const_spec((1, D_MODEL))],
        out_specs=seq,
        out_shape=jax.ShapeDtypeStruct(h.shape, _F32),
        scratch_shapes=[pltpu.VMEM((tm, D_FF), _BF16), pltpu.VMEM((tm, D_MODEL), _BF16)],
        compiler_params=pltpu.CompilerParams(dimension_semantics=("arbitrary", "arbitrary"),
                                             vmem_limit_bytes=VMEM_LIMIT),
        name="attn_ffn",
    )(h, kt, v, gx, wq, wo, g2, wgu, wd, gf)


def _row(vec):
    return vec.reshape(1, -1).astype(_F32)


def _trunk(x, mem, p):
    bsz, length, _ = x.shape
    h1, v, zs, gate = _ffn_inproj(x.reshape(bsz * length, D_MODEL), p['ffn1_g'], p['ffn1_wgu'],
                                  p['ffn1_wd'], p['mix_g'], p['w_in'], p['b_in'])
    seq = lambda a: a.reshape(bsz, length, a.shape[-1])
    h1, v, zs, gate = seq(h1), seq(v), seq(zs), seq(gate)
    ys = _ssm_scan(zs, p['ssm_w'], p['ssm_mv'], p['ssm_coef'])
    h2 = _mix(v, zs, ys, gate, h1, p['conv_w'], p['conv_b'], p['conv_ln_g'], p['conv_ln_b'],
              p['conv_w_pw'], p['ssm_d'], p['ssm_w_glu'], p['w_out'])
    kt, vm = _mem_kv(mem, p['mem_g'], p['xattn_wkt'], p['xattn_wv'])
    return _attn_ffn(h2, kt, vm, p['xattn_g'], p['xattn_wq'], p['xattn_wo'], p['ffn2_g'],
                     p['ffn2_wgu'], p['ffn2_wd'], p['final_g'])


def kernel(x_prompt, x_sample, mem_prompt, mem_sample, ffn1_g, ffn1_wgu, ffn1_wd, mix_g, w_in, b_in, conv_w, conv_b, conv_ln_g, conv_ln_b, conv_w_pw, ssm_lam_re, ssm_lam_im, ssm_log_dt, ssm_b_re, ssm_b_im, ssm_c_re, ssm_c_im, ssm_d, ssm_w_glu, w_out, xattn_g, mem_g, xattn_wq, xattn_wkv, xattn_wo, ffn2_g, ffn2_wgu, ffn2_wd, final_g):
    assert ffn1_g.shape[0] == 1, "single-layer trunk"
    p = {}
    p['ffn1_g'], p['mix_g'], p['xattn_g'] = _row(ffn1_g[0]), _row(mix_g[0]), _row(xattn_g[0])
    p['mem_g'], p['ffn2_g'], p['final_g'] = _row(mem_g[0]), _row(ffn2_g[0]), _row(final_g)
    p['ffn1_wgu'], p['ffn2_wgu'] = ffn1_wgu[0].astype(_BF16), ffn2_wgu[0].astype(_BF16)
    p['ffn1_wd'], p['ffn2_wd'] = ffn1_wd[0].astype(_BF16), ffn2_wd[0].astype(_BF16)
    p['w_in'], p['b_in'] = w_in[0].astype(_BF16), _row(b_in[0])
    p['conv_w'], p['conv_b'] = conv_w[0].astype(_F32), _row(conv_b[0])
    p['conv_ln_g'], p['conv_ln_b'] = _row(conv_ln_g[0]), _row(conv_ln_b[0])
    p['conv_w_pw'] = conv_w_pw[0].astype(_BF16)
    p['ssm_w'], p['ssm_mv'], p['ssm_coef'] = _ssm_tables(
        ssm_lam_re[0], ssm_lam_im[0], ssm_log_dt[0], ssm_b_re[0], ssm_b_im[0],
        ssm_c_re[0], ssm_c_im[0])
    p['ssm_d'], p['ssm_w_glu'] = _row(ssm_d[0]), ssm_w_glu[0].astype(_BF16)
    p['w_out'] = w_out[0].astype(_BF16)
    p['xattn_wq'], p['xattn_wo'] = xattn_wq[0].astype(_BF16), xattn_wo[0].astype(_BF16)
    p['xattn_wkt'] = xattn_wkv[0][:, :D_MODEL].T.astype(_BF16)
    p['xattn_wv'] = xattn_wkv[0][:, D_MODEL:].astype(_BF16)
    return (_trunk(x_prompt, mem_prompt, p), _trunk(x_sample, mem_sample, p))
```

```python
import functools

import jax
import jax.numpy as jnp
import numpy as np
from jax import lax
from jax.experimental import pallas as pl
from jax.experimental.pallas import tpu as pltpu

D_MODEL = 1024
D_FF = 2816
FF_CHUNK = 256
N_FF_CHUNKS = D_FF // FF_CHUNK
CONV_WIDTH = 512
CONV_K = 31
CONV_PAD = CONV_K // 2
SUBLANES = 8
LANES = 128
HALO = 2 * SUBLANES
CONV_ROWS = 64
CONV_COLS = 256
SSM_WIDTH = 512
SSM_GROUP = 16
SSM_GROUPS = SSM_WIDTH // SSM_GROUP
SSM_STATE = 64
CHUNK = 16
FLAT = CHUNK * SSM_GROUP
GROUPS_PER_BLOCK = LANES // SSM_GROUP
RELAYOUT_ROWS_IN = 64
RELAYOUT_ROWS_OUT = 32
SCAN_STEPS = 8
POWER_ROWS = 24
N_MEM = 256
X_HEADS = 4
X_HEAD_DIM = D_MODEL // X_HEADS
EPS = 1e-6
IN_COLS = 2 * CONV_WIDTH + SSM_WIDTH + 2 * D_MODEL

ROW_TILE = 512
FFN_ROW_TILE = 512
VMEM_LIMIT = 60 * 1024 * 1024
SCAN_VMEM_BUDGET = 44 * 1024 * 1024

_F32 = jnp.float32
_BF16 = jnp.bfloat16


def _dot(a, b):
    return jnp.dot(a, b, preferred_element_type=_F32)


def _rmsnorm(x, g):
    return x * lax.rsqrt(jnp.mean(x * x, axis=-1, keepdims=True) + EPS) * g


def _const_spec(shape):
    nd = len(shape)
    return pl.BlockSpec(shape, lambda *_: (0,) * nd, pipeline_mode=pl.Buffered(1))


def _swiglu(xn, wgu_ref, wd_ref, act_ref):
    for j in range(N_FF_CHUNKS):
        lo = j * FF_CHUNK
        g = _dot(xn, wgu_ref[:, lo:lo + FF_CHUNK])
        u = _dot(xn, wgu_ref[:, D_FF + lo:D_FF + lo + FF_CHUNK])
        act_ref[:, lo:lo + FF_CHUNK] = (g * jax.nn.sigmoid(g) * u).astype(_BF16)
    return _dot(act_ref[...], wd_ref[...])


def _ffn_inproj_kernel(x_ref, g1_ref, wgu_ref, wd_ref, gmix_ref, win_ref, bin_ref,
                       h1_ref, v_ref, zs_ref, gate_ref, act_ref):
    x = x_ref[...]
    xn = _rmsnorm(x, g1_ref[...]).astype(_BF16)
    h1 = x + 0.5 * _swiglu(xn, wgu_ref, wd_ref, act_ref)
    h1_ref[...] = h1
    un = _rmsnorm(h1, gmix_ref[...]).astype(_BF16)
    z = _dot(un, win_ref[...]) + bin_ref[...]
    v_ref[...] = z[:, :CONV_WIDTH] * jax.nn.sigmoid(z[:, CONV_WIDTH:2 * CONV_WIDTH])
    zs_ref[...] = z[:, 2 * CONV_WIDTH:2 * CONV_WIDTH + SSM_WIDTH]
    gate_ref[...] = jax.nn.sigmoid(z[:, 2 * CONV_WIDTH + SSM_WIDTH:])


def _ffn_inproj(x, g1, wgu, wd, gmix, win, b_in):
    rows = x.shape[0]
    tm = FFN_ROW_TILE
    row = lambda w: pl.BlockSpec((tm, w), lambda i: (i, 0))
    return pl.pallas_call(
        _ffn_inproj_kernel,
        grid=(rows // tm,),
        in_specs=[row(D_MODEL), _const_spec((1, D_MODEL)),
                  _const_spec(wgu.shape), _const_spec(wd.shape),
                  _const_spec((1, D_MODEL)), _const_spec(win.shape), _const_spec((1, IN_COLS))],
        out_specs=[row(D_MODEL), row(CONV_WIDTH), row(SSM_WIDTH), row(2 * D_MODEL)],
        out_shape=[jax.ShapeDtypeStruct((rows, D_MODEL), _F32),
                   jax.ShapeDtypeStruct((rows, CONV_WIDTH), _F32),
                   jax.ShapeDtypeStruct((rows, SSM_WIDTH), _F32),
                   jax.ShapeDtypeStruct((rows, 2 * D_MODEL), _F32)],
        scratch_shapes=[pltpu.VMEM((tm, D_FF), _BF16)],
        compiler_params=pltpu.CompilerParams(dimension_semantics=("arbitrary",),
                                             vmem_limit_bytes=VMEM_LIMIT),
        name="ffn_inproj",
    )(x, g1, wgu, wd, gmix, win, b_in)


def _piece_masks(shape):
    piece = lax.broadcasted_iota(jnp.int32, shape, 1) // SSM_GROUP
    return [piece == p for p in range(1, GROUPS_PER_BLOCK)]


def _diagonal_select(xs, masks):
    nb = GROUPS_PER_BLOCK
    out = []
    for k in range(nb):
        acc = xs[-k % nb]
        for p in range(1, nb):
            acc = jnp.where(masks[p - 1], xs[(p - k) % nb], acc)
        out.append(acc)
    return out


def _ssm_scan_kernel(z_ref, w_ref, mv_ref, coef_ref, y_ref, a_ref, sf_ref, sb_ref, yy_ref,
                     *, n_chunks):
    nb = GROUPS_PER_BLOCK
    halves = CHUNK // nb

    def relayout_in(i, carry):
        rr = RELAYOUT_ROWS_IN
        row0 = pl.multiple_of(i * (rr * CHUNK), rr * CHUNK)
        crow = pl.multiple_of(i * rr, rr)
        masks = _piece_masks((rr // 2, LANES))
        for half in range(halves):
            xs = [pltpu.bitcast(z_ref[pl.ds(row0 + half * nb + t, rr, stride=CHUNK), :].astype(_BF16),
                                jnp.uint32) for t in range(nb)]
            xs = [x if t == 0 else pltpu.roll(x, SSM_GROUP * t, 1) for t, x in enumerate(xs)]
            for g, x in enumerate(_diagonal_select(xs, masks)):
                a_ref[g, pl.ds(crow, rr), half * LANES:(half + 1) * LANES] = pltpu.bitcast(x, _BF16)
        return carry

    lax.fori_loop(0, n_chunks // RELAYOUT_ROWS_IN, relayout_in, 0)

    for g in range(nb):
        s = _dot(a_ref[g], w_ref[g])
        packed = pl.ds(g, n_chunks, stride=nb)
        for d, s_ref in enumerate((sf_ref, sb_ref)):
            own = s[:, d * LANES:(d + 1) * LANES]
            s_ref[packed, :] = own
            yy_ref[d, packed, :] = pltpu.roll(own, SSM_STATE, 1)

    n_blocks = n_chunks // SCAN_STEPS
    span = SCAN_STEPS * nb
    s_refs = (sf_ref, sb_ref)
    coefs = [(coef_ref[3 * d], coef_ref[3 * d + 1], coef_ref[3 * d + 2]) for d in range(2)]

    def rows(blk):
        return pl.ds(pl.multiple_of(blk * span, span), span)

    def load(d, blk):
        return s_refs[d][rows(blk), :], yy_ref[d, rows(blk), :]

    def advance(d, blk, state):
        v, w, s_all, sw_all = state
        a, b, bp = coefs[d]
        entering = [None] * SCAN_STEPS
        for k in (range(SCAN_STEPS) if d == 0 else range(SCAN_STEPS - 1, -1, -1)):
            entering[k] = v
            own = slice(k * nb, (k + 1) * nb)
            v, w = a * v + b * w + s_all[own], a * w + bp * v + sw_all[own]
        s_refs[d][rows(blk), :] = jnp.concatenate(entering, axis=0)
        return v, w

    def scan_body(i, carry):
        fwd, bwd = carry
        blk_f, blk_b = i, n_blocks - 1 - i
        next_f = load(0, jnp.minimum(blk_f + 1, n_blocks - 1))
        next_b = load(1, jnp.maximum(blk_b - 1, 0))
        return advance(0, blk_f, fwd) + next_f, advance(1, blk_b, bwd) + next_b

    zero = jnp.zeros((nb, LANES), _F32)
    lax.fori_loop(0, n_blocks, scan_body,
                  ((zero, zero) + load(0, 0), (zero, zero) + load(1, n_blocks - 1)))

    for g in range(nb):
        lhs = jnp.concatenate([a_ref[g],
                               sf_ref[pl.ds(g, n_chunks, stride=nb), :].astype(_BF16),
                               sb_ref[pl.ds(g, n_chunks, stride=nb), :].astype(_BF16)], axis=1)
        yg = _dot(lhs, mv_ref[g])
        for half in range(halves):
            yy_ref[half, g * n_chunks:(g + 1) * n_chunks, :] = yg[:, half * LANES:(half + 1) * LANES]

    def relayout_out(i, carry):
        rr = RELAYOUT_ROWS_OUT
        row0 = pl.multiple_of(i * (rr * CHUNK), rr * CHUNK)
        crow = pl.multiple_of(i * rr, rr)
        masks = _piece_masks((rr, LANES))
        for half in range(halves):
            ys = [yy_ref[half, pl.ds(g * n_chunks + crow, rr), :] for g in range(nb)]
            for t, y in enumerate(_diagonal_select(ys, masks)):
                if t:
                    y = pltpu.roll(y, LANES - SSM_GROUP * t, 1)
                y_ref[pl.ds(row0 + half * nb + t, rr, stride=CHUNK), :] = y
        return carry

    lax.fori_loop(0, n_chunks // RELAYOUT_ROWS_OUT, relayout_out, 0)


def _ssm_scan(zs, w, mv, coef):
    bsz, length, _ = zs.shape
    n_chunks = length // CHUNK
    nb = GROUPS_PER_BLOCK
    block_bytes = length * LANES * 4
    scratch_bytes = n_chunks * nb * (FLAT * 2 + 2 * LANES * 4 + FLAT * 4)
    table_bytes = nb * (FLAT * FLAT + 2 * FLAT * FLAT) * 2
    double = 4 * block_bytes + scratch_bytes + table_bytes <= SCAN_VMEM_BUDGET
    seq_mode = pl.Buffered(2 if double else 1)
    once = pl.Buffered(1)
    seq_spec = pl.BlockSpec((None, length, LANES), lambda j, b: (b, 0, j), pipeline_mode=seq_mode)
    return pl.pallas_call(
        functools.partial(_ssm_scan_kernel, n_chunks=n_chunks),
        grid=(SSM_GROUPS // nb, bsz),
        in_specs=[seq_spec,
                  pl.BlockSpec((nb, FLAT, FLAT), lambda j, b: (j, 0, 0), pipeline_mode=once),
                  pl.BlockSpec((nb, 2 * FLAT, FLAT), lambda j, b: (j, 0, 0), pipeline_mode=once),
                  pl.BlockSpec((None, 6, nb, LANES), lambda j, b: (j, 0, 0, 0), pipeline_mode=once)],
        out_specs=seq_spec,
        out_shape=jax.ShapeDtypeStruct(zs.shape, _F32),
        scratch_shapes=[pltpu.VMEM((nb, n_chunks, FLAT), _BF16),
                        pltpu.VMEM((n_chunks * nb, LANES), _F32),
                        pltpu.VMEM((n_chunks * nb, LANES), _F32),
                        pltpu.VMEM((2, n_chunks * nb, LANES), _F32)],
        compiler_params=pltpu.CompilerParams(dimension_semantics=("arbitrary", "arbitrary"),
                                             vmem_limit_bytes=VMEM_LIMIT),
        name="ssm_scan",
    )(zs, w, mv, coef)


def _exact_dot(a, b):
    return jnp.dot(a, b, preferred_element_type=_F32, precision=lax.Precision.HIGHEST)


def _lane_window(cols, start):
    first, r = divmod(start, LANES)
    if r == 0:
        return cols[first]
    lane = lax.broadcasted_iota(jnp.int32, cols[first].shape, 1)
    return jnp.where(lane < LANES - r, pltpu.roll(cols[first], LANES - r, 1),
                     pltpu.roll(cols[first + 1], LANES - r, 1))


def _ssm_tables_kernel(pa_ref, pb_ref, ba_ref, bb_ref, ct_ref, pt_ref, bt_ref,
                       tile_ref, sel_ref, w_ref, mv_ref):
    nb = GROUPS_PER_BLOCK
    shift = (pl.program_id(0) % nb) * SSM_GROUP

    def rotated_block(s):
        return (s // nb) * nb + (s % nb + pl.program_id(0) % nb) % nb

    def rotate_pieces(x):
        return jnp.concatenate([pltpu.roll(x[:, :LANES], shift, 1),
                                pltpu.roll(x[:, LANES:], shift, 1)], axis=1)

    def rows_of(s):
        return pl.ds(pl.multiple_of(rotated_block(s) * SSM_GROUP, SSM_GROUP), SSM_GROUP)

    lag_stacks = []
    for d in range(2):
        for s in range(CHUNK):
            k = CHUNK - 1 - s if d == 0 else s
            blk = pa_ref[d, k:k + 1, :] * ba_ref[d] + pb_ref[d, k:k + 1, :] * bb_ref[d]
            w_ref[rows_of(s), d * LANES:(d + 1) * LANES] = blk.astype(_BF16)

        c_re = _exact_dot(ct_ref[d, 0], tile_ref[...])
        c_im = _exact_dot(ct_ref[d, 1], tile_ref[...])

        def c_times_power(which):
            p_re = _exact_dot(pt_ref[d, 0], sel_ref[which])
            p_im = _exact_dot(pt_ref[d, 1], sel_ref[which])
            return c_re * p_re - c_im * p_im, c_re * p_im + c_im * p_re

        v_re, v_im = c_times_power(d)
        base = FLAT + d * LANES
        mv_ref[base:base + SSM_STATE, :] = rotate_pieces(v_re).astype(_BF16)
        mv_ref[base + SSM_STATE:base + LANES, :] = rotate_pieces(-v_im).astype(_BF16)
        k_re, k_im = c_times_power(2 + d)
        lag_stacks.append(_exact_dot(bt_ref[d, 0], k_re) - _exact_dot(bt_ref[d, 1], k_im))

    zero = jnp.zeros((SSM_GROUP, LANES), _F32)
    fwd = [zero, zero, lag_stacks[0][:, :LANES], lag_stacks[0][:, LANES:], zero]
    bwd = [lag_stacks[1][:, :LANES], lag_stacks[1][:, LANES:], zero, zero, zero]
    for s in range(CHUNK):
        f0, b0 = FLAT - SSM_GROUP * s, SSM_GROUP * (CHUNK - 1 - s)
        blk = jnp.concatenate([_lane_window(fwd, f0) + _lane_window(bwd, b0),
                               _lane_window(fwd, f0 + LANES) + _lane_window(bwd, b0 + LANES)], axis=1)
        mv_ref[rows_of(s), :] = rotate_pieces(blk).astype(_BF16)


def _table_constants():
    t = np.arange(FLAT) // SSM_GROUP
    o = np.arange(FLAT) % SSM_GROUP
    tile = (np.arange(SSM_GROUP)[:, None] == o[None, :])
    k = np.arange(LANES)[:, None]
    sel = np.stack([k == t + 1, k == CHUNK - t, k == t, k == CHUNK - 1 - t])
    return jnp.asarray(tile, _F32), jnp.asarray(sel, _F32)


def _ssm_tables(lam_re, lam_im, log_dt, b_re, b_im, c_re, c_im):
    G, N = SSM_GROUPS, SSM_STATE
    dt = jnp.exp(log_dt)[..., None]
    mag = jnp.exp(lam_re * dt)
    lr, li = mag * jnp.cos(lam_im * dt), mag * jnp.sin(lam_im * dt)
    p_re, p_im = lr[:, :, None], li[:, :, None]
    while p_re.shape[2] < CHUNK:
        top_re, top_im = p_re[:, :, -1:], p_im[:, :, -1:]
        p_re, p_im = (jnp.concatenate([p_re, p_re * top_re - p_im * top_im], axis=2),
                      jnp.concatenate([p_im, p_re * top_im + p_im * top_re], axis=2))
    p_re = jnp.concatenate([jnp.ones_like(lr)[:, :, None], p_re], axis=2)
    p_im = jnp.concatenate([jnp.zeros_like(li)[:, :, None], p_im], axis=2)
    den = lam_re * lam_re + lam_im * lam_im
    qr = ((lr - 1.0) * lam_re + li * lam_im) / den
    qi = (li * lam_re - (lr - 1.0) * lam_im) / den
    bt_re = (qr[..., None] * b_re - qi[..., None] * b_im).transpose(0, 1, 3, 2)
    bt_im = (qr[..., None] * b_im + qi[..., None] * b_re).transpose(0, 1, 3, 2)
    pad_k = lambda x: jnp.pad(x, ((0, 0), (0, 0), (0, POWER_ROWS - CHUNK - 1), (0, 0)))
    per_group = lambda x: jnp.moveaxis(x, 1, 0)
    pa = per_group(pad_k(jnp.concatenate([p_re, p_im], -1)))
    pb = per_group(pad_k(jnp.concatenate([-p_im, p_re], -1)))
    ba = per_group(jnp.concatenate([bt_re, bt_re], -1))
    bb = per_group(jnp.concatenate([bt_im, bt_im], -1))
    ct = per_group(jnp.stack([c_re, c_im], axis=2).transpose(0, 1, 2, 4, 3))
    pt = jnp.stack([p_re, p_im], axis=2).transpose(0, 1, 2, 4, 3)
    pt = per_group(jnp.pad(pt, ((0, 0),) * 4 + ((0, LANES - CHUNK - 1),)))
    bt = per_group(jnp.stack([bt_re, bt_im], axis=2))
    tile, sel = _table_constants()
    group = lambda *tail: pl.BlockSpec((None,) + tail, lambda g: (g,) + (0,) * len(tail))
    const = lambda a: pl.BlockSpec(a.shape, lambda g: (0,) * a.ndim, pipeline_mode=pl.Buffered(1))
    w, mv = pl.pallas_call(
        _ssm_tables_kernel,
        grid=(G,),
        in_specs=[group(2, POWER_ROWS, LANES), group(2, POWER_ROWS, LANES),
                  group(2, SSM_GROUP, LANES), group(2, SSM_GROUP, LANES),
                  group(2, 2, N, SSM_GROUP), group(2, 2, N, LANES), group(2, 2, SSM_GROUP, N),
                  const(tile), const(sel)],
        out_specs=[group(FLAT, FLAT), group(2 * FLAT, FLAT)],
        out_shape=[jax.ShapeDtypeStruct((G, FLAT, FLAT), _BF16),
                   jax.ShapeDtypeStruct((G, 2 * FLAT, FLAT), _BF16)],
        compiler_params=pltpu.CompilerParams(dimension_semantics=("arbitrary",)),
        name="ssm_tables",
    )(pa, pb, ba, bb, ct, pt, bt, tile, sel)
    fr, fi = p_re[:, :, CHUNK], p_im[:, :, CHUNK]
    coef = jnp.stack([jnp.concatenate([fr, fr], -1), jnp.concatenate([-fi, fi], -1),
                      jnp.concatenate([fi, -fi], -1)], axis=1)
    coef = coef.reshape(6, G // GROUPS_PER_BLOCK, GROUPS_PER_BLOCK, 2 * N).transpose(1, 0, 2, 3)
    return w, mv, coef


def _depthwise_conv(buf_ref, cw_ref, c_ref, tm):
    n_tiles = CONV_ROWS // SUBLANES
    sublane = lax.broadcasted_iota(jnp.int32, (SUBLANES, CONV_COLS), 0)

    def rows(i, carry):
        r0 = pl.multiple_of(i * CONV_ROWS, CONV_ROWS)
        for c0 in range(0, CONV_WIDTH, CONV_COLS):
            cols = slice(c0, c0 + CONV_COLS)
            out = [None] * n_tiles
            for r in range(SUBLANES):
                part = None
                for m in range(HALO * 2 // SUBLANES):
                    k = SUBLANES * m + r - 1
                    if 0 <= k < CONV_K:
                        term = (buf_ref[pl.ds(r0 + SUBLANES * m, CONV_ROWS + SUBLANES), cols]
                                * cw_ref[k:k + 1, cols])
                        part = term if part is None else part + term
                tiles = [part[SUBLANES * j:SUBLANES * (j + 1)] for j in range(n_tiles + 1)]
                for j in range(n_tiles):
                    if r == 0:
                        shifted = tiles[j]
                    else:
                        shifted = pltpu.roll(jnp.where(sublane < r, tiles[j + 1], tiles[j]),
                                             SUBLANES - r, 0)
                    out[j] = shifted if out[j] is None else out[j] + shifted
            c_ref[pl.ds(r0, CONV_ROWS), cols] = jnp.concatenate(out, axis=0)
        return carry

    lax.fori_loop(0, tm // CONV_ROWS, rows, 0)


def _mix_kernel(v_ref, vp_ref, vn_ref, zs_ref, ys_ref, gate_ref, h1_ref,
                cw_ref, cb_ref, lng_ref, lnb_ref, wpw_ref, d_ref, wglu_ref, wout_ref,
                h2_ref, buf_ref, c_ref, *, tm):
    i = pl.program_id(1)
    last = pl.num_programs(1) - 1
    buf_ref[0:HALO, :] = jnp.where(i > 0, vp_ref[...], 0.0)
    buf_ref[HALO:HALO + tm, :] = v_ref[...]
    buf_ref[HALO + tm:2 * HALO + tm, :] = jnp.where(i < last, vn_ref[...], 0.0)
    _depthwise_conv(buf_ref, cw_ref, c_ref, tm)
    c = c_ref[...] + cb_ref[...]
    c = c - jnp.mean(c, axis=-1, keepdims=True)
    c = c * lax.rsqrt(jnp.mean(c * c, axis=-1, keepdims=True) + EPS) * lng_ref[...] + lnb_ref[...]
    conv_out = _dot((c * jax.nn.sigmoid(c)).astype(_BF16), wpw_ref[...])

    y = ys_ref[...] + d_ref[...] * zs_ref[...]
    ag = _dot(jax.nn.gelu(y).astype(_BF16), wglu_ref[...])
    ssm_out = ag[:, :D_MODEL] * jax.nn.sigmoid(ag[:, D_MODEL:])

    gate = gate_ref[...]
    merged = gate[:, :D_MODEL] * conv_out + gate[:, D_MODEL:] * ssm_out
    h2_ref[...] = h1_ref[...] + _dot(merged.astype(_BF16), wout_ref[...])


def _mix(v, zs, ys, gate, h1, cw, cb, lng, lnb, wpw, d, wglu, wout):
    bsz, length, _ = v.shape
    tm = ROW_TILE
    per = tm // HALO
    n_halo = length // HALO
    seq = lambda w: pl.BlockSpec((None, tm, w), lambda b, i: (b, i, 0))
    prev = pl.BlockSpec((None, HALO, CONV_WIDTH), lambda b, i: (b, jnp.maximum(i * per - 1, 0), 0))
    nxt = pl.BlockSpec((None, HALO, CONV_WIDTH),
                       lambda b, i: (b, jnp.minimum((i + 1) * per, n_halo - 1), 0))
    return pl.pallas_call(
        functools.partial(_mix_kernel, tm=tm),
        grid=(bsz, length // tm),
        in_specs=[seq(CONV_WIDTH), prev, nxt, seq(SSM_WIDTH), seq(SSM_WIDTH), seq(2 * D_MODEL),
                  seq(D_MODEL),
                  _const_spec(cw.shape), _const_spec((1, CONV_WIDTH)), _const_spec((1, CONV_WIDTH)),
                  _const_spec((1, CONV_WIDTH)), _const_spec(wpw.shape), _const_spec((1, SSM_WIDTH)),
                  _const_spec(wglu.shape), _const_spec(wout.shape)],
        out_specs=seq(D_MODEL),
        out_shape=jax.ShapeDtypeStruct((bsz, length, D_MODEL), _F32),
        scratch_shapes=[pltpu.VMEM((tm + 2 * HALO, CONV_WIDTH), _F32),
                        pltpu.VMEM((tm, CONV_WIDTH), _F32)],
        compiler_params=pltpu.CompilerParams(dimension_semantics=("arbitrary", "arbitrary"),
                                             vmem_limit_bytes=VMEM_LIMIT),
        name="mix",
    )(v, v, v, zs, ys, gate, h1, cw, cb, lng, lnb, wpw, d, wglu, wout)


def _mem_kv_kernel(mem_ref, g_ref, wkt_ref, wv_ref, kt_ref, v_ref):
    m = _rmsnorm(mem_ref[...], g_ref[...]).astype(_BF16)
    kt = lax.dot_general(wkt_ref[...], m, (((1,), (1,)), ((), ())), preferred_element_type=_F32)
    kt_ref[...] = kt.astype(_BF16)
    v_ref[...] = _dot(m, wv_ref[...]).astype(_BF16)


def _mem_kv(mem, g, wkt, wv):
    bsz = mem.shape[0]
    return pl.pallas_call(
        _mem_kv_kernel,
        grid=(bsz,),
        in_specs=[pl.BlockSpec((None, N_MEM, D_MODEL), lambda b: (b, 0, 0)),
                  _const_spec((1, D_MODEL)), _const_spec(wkt.shape), _const_spec(wv.shape)],
        out_specs=[pl.BlockSpec((None, D_MODEL, N_MEM), lambda b: (b, 0, 0)),
                   pl.BlockSpec((None, N_MEM, D_MODEL), lambda b: (b, 0, 0))],
        out_shape=[jax.ShapeDtypeStruct((bsz, D_MODEL, N_MEM), _BF16),
                   jax.ShapeDtypeStruct((bsz, N_MEM, D_MODEL), _BF16)],
        compiler_params=pltpu.CompilerParams(dimension_semantics=("arbitrary",)),
        name="mem_kv",
    )(mem, g, wkt, wv)


def _attn_ffn_kernel(h_ref, kt_ref, v_ref, gx_ref, wq_ref, wo_ref, g2_ref, wgu_ref, wd_ref,
                     gf_ref, out_ref, act_ref, o_ref):
    h = h_ref[...]
    q = _dot(_rmsnorm(h, gx_ref[...]).astype(_BF16), wq_ref[...]).astype(_BF16)
    for hd in range(X_HEADS):
        sl = slice(hd * X_HEAD_DIM, (hd + 1) * X_HEAD_DIM)
        s = _dot(q[:, sl], kt_ref[sl, :]) * (X_HEAD_DIM ** -0.5)
        e = jnp.exp(s - jnp.max(s, axis=-1, keepdims=True))
        p = e / jnp.sum(e, axis=-1, keepdims=True)
        o_ref[:, sl] = _dot(p.astype(_BF16), v_ref[:, sl]).astype(_BF16)
    h = h + _dot(o_ref[...], wo_ref[...])
    xn = _rmsnorm(h, g2_ref[...]).astype(_BF16)
    h = h + 0.5 * _swiglu(xn, wgu_ref, wd_ref, act_ref)
    out_ref[...] = _rmsnorm(h, gf_ref[...])


def _attn_ffn(h, kt, v, gx, wq, wo, g2, wgu, wd, gf):
    bsz, length, _ = h.shape
    tm = ROW_TILE
    seq = pl.BlockSpec((None, tm, D_MODEL), lambda b, i: (b, i, 0))
    return pl.pallas_call(
        _attn_ffn_kernel,
        grid=(bsz, length // tm),
        in_specs=[seq,
                  pl.BlockSpec((None, D_MODEL, N_MEM), lambda b, i: (b, 0, 0)),
                  pl.BlockSpec((None, N_MEM, D_MODEL), lambda b, i: (b, 0, 0)),
                  _const_spec((1, D_MODEL)), _const_spec(wq.shape), _const_spec(wo.shape),
                  _const_spec((1, D_MODEL)), _const_spec(wgu.shape), _const_spec(wd.shape),
                  _const_spec((1, D_MODEL))],
        out_specs=seq,
        out_shape=jax.ShapeDtypeStruct(h.shape, _F32),
        scratch_shapes=[pltpu.VMEM((tm, D_FF), _BF16), pltpu.VMEM((tm, D_MODEL), _BF16)],
        compiler_params=pltpu.CompilerParams(dimension_semantics=("arbitrary", "arbitrary"),
                                             vmem_limit_bytes=VMEM_LIMIT),
        name="attn_ffn",
    )(h, kt, v, gx, wq, wo, g2, wgu, wd, gf)


def _row(vec):
    return vec.reshape(1, -1).astype(_F32)


def _trunk(x, mem, p):
    bsz, length, _ = x.shape
    h1, v, zs, gate = _ffn_inproj(x.reshape(bsz * length, D_MODEL), p['ffn1_g'], p['ffn1_wgu'],
                                  p['ffn1_wd'], p['mix_g'], p['w_in'], p['b_in'])
    seq = lambda a: a.reshape(bsz, length, a.shape[-1])
    h1, v, zs, gate = seq(h1), seq(v), seq(zs), seq(gate)
    ys = _ssm_scan(zs, p['ssm_w'], p['ssm_mv'], p['ssm_coef'])
    h2 = _mix(v, zs, ys, gate, h1, p['conv_w'], p['conv_b'], p['conv_ln_g'], p['conv_ln_b'],
              p['conv_w_pw'], p['ssm_d'], p['ssm_w_glu'], p['w_out'])
    kt, vm = _mem_kv(mem, p['mem_g'], p['xattn_wkt'], p['xattn_wv'])
    return _attn_ffn(h2, kt, vm, p['xattn_g'], p['xattn_wq'], p['xattn_wo'], p['ffn2_g'],
                     p['ffn2_wgu'], p['ffn2_wd'], p['final_g'])


def kernel(x_prompt, x_sample, mem_prompt, mem_sample, ffn1_g, ffn1_wgu, ffn1_wd, mix_g, w_in, b_in, conv_w, conv_b, conv_ln_g, conv_ln_b, conv_w_pw, ssm_lam_re, ssm_lam_im, ssm_log_dt, ssm_b_re, ssm_b_im, ssm_c_re, ssm_c_im, ssm_d, ssm_w_glu, w_out, xattn_g, mem_g, xattn_wq, xattn_wkv, xattn_wo, ffn2_g, ffn2_wgu, ffn2_wd, final_g):
    assert ffn1_g.shape[0] == 1, "single-layer trunk"
    p = {}
    p['ffn1_g'], p['mix_g'], p['xattn_g'] = _row(ffn1_g[0]), _row(mix_g[0]), _row(xattn_g[0])
    p['mem_g'], p['ffn2_g'], p['final_g'] = _row(mem_g[0]), _row(ffn2_g[0]), _row(final_g)
    p['ffn1_wgu'], p['ffn2_wgu'] = ffn1_wgu[0].astype(_BF16), ffn2_wgu[0].astype(_BF16)
    p['ffn1_wd'], p['ffn2_wd'] = ffn1_wd[0].astype(_BF16), ffn2_wd[0].astype(_BF16)
    p['w_in'], p['b_in'] = w_in[0].astype(_BF16), _row(b_in[0])
    p['conv_w'], p['conv_b'] = conv_w[0].astype(_F32), _row(conv_b[0])
    p['conv_ln_g'], p['conv_ln_b'] = _row(conv_ln_g[0]), _row(conv_ln_b[0])
    p['conv_w_pw'] = conv_w_pw[0].astype(_BF16)
    p['ssm_w'], p['ssm_mv'], p['ssm_coef'] = _ssm_tables(
        ssm_lam_re[0], ssm_lam_im[0], ssm_log_dt[0], ssm_b_re[0], ssm_b_im[0],
        ssm_c_re[0], ssm_c_im[0])
    p['ssm_d'], p['ssm_w_glu'] = _row(ssm_d[0]), ssm_w_glu[0].astype(_BF16)
    p['w_out'] = w_out[0].astype(_BF16)
    p['xattn_wq'], p['xattn_wo'] = xattn_wq[0].astype(_BF16), xattn_wo[0].astype(_BF16)
    p['xattn_wkt'] = xattn_wkv[0][:, :D_MODEL].T.astype(_BF16)
    p['xattn_wv'] = xattn_wkv[0][:, D_MODEL:].astype(_BF16)
    return (_trunk(x_prompt, mem_prompt, p), _trunk(x_sample, mem_sample, p))
```

```python
import functools

import jax
import jax.numpy as jnp
import numpy as np
from jax import lax
from jax.experimental import pallas as pl
from jax.experimental.pallas import tpu as pltpu

D_MODEL = 1024
D_FF = 2816
FF_CHUNK = 256
N_FF_CHUNKS = D_FF // FF_CHUNK
CONV_WIDTH = 512
CONV_K = 31
CONV_PAD = CONV_K // 2
SUBLANES = 8
LANES = 128
HALO = 2 * SUBLANES
CONV_ROWS = 64
CONV_COLS = 256
SSM_WIDTH = 512
SSM_GROUP = 16
SSM_GROUPS = SSM_WIDTH // SSM_GROUP
SSM_STATE = 64
CHUNK = 16
FLAT = CHUNK * SSM_GROUP
GROUPS_PER_BLOCK = LANES // SSM_GROUP
RELAYOUT_ROWS_IN = 64
RELAYOUT_ROWS_OUT = 32
SCAN_STEPS = 8
POWER_ROWS = 24
N_MEM = 256
X_HEADS = 4
X_HEAD_DIM = D_MODEL // X_HEADS
EPS = 1e-6
IN_COLS = 2 * CONV_WIDTH + SSM_WIDTH + 2 * D_MODEL

ROW_TILE = 512
FFN_ROW_TILE = 512
ATTN_ROW_TILE = 1024
VMEM_LIMIT = 60 * 1024 * 1024
SCAN_VMEM_BUDGET = 44 * 1024 * 1024

_F32 = jnp.float32
_BF16 = jnp.bfloat16


def _dot(a, b):
    return jnp.dot(a, b, preferred_element_type=_F32)


def _rmsnorm(x, g):
    return x * lax.rsqrt(jnp.mean(x * x, axis=-1, keepdims=True) + EPS) * g


def _const_spec(shape):
    nd = len(shape)
    return pl.BlockSpec(shape, lambda *_: (0,) * nd, pipeline_mode=pl.Buffered(1))


def _swiglu(xn, wgu_ref, wd_ref, act_ref):
    for j in range(N_FF_CHUNKS):
        lo = j * FF_CHUNK
        g = _dot(xn, wgu_ref[:, lo:lo + FF_CHUNK])
        u = _dot(xn, wgu_ref[:, D_FF + lo:D_FF + lo + FF_CHUNK])
        act_ref[:, lo:lo + FF_CHUNK] = (g * jax.nn.sigmoid(g) * u).astype(_BF16)
    return _dot(act_ref[...], wd_ref[...])


def _ffn_inproj_kernel(x_ref, g1_ref, wgu_ref, wd_ref, gmix_ref, win_ref, bin_ref,
                       h1_ref, v_ref, zs_ref, gate_ref, act_ref):
    x = x_ref[...]
    xn = _rmsnorm(x, g1_ref[...]).astype(_BF16)
    h1 = x + 0.5 * _swiglu(xn, wgu_ref, wd_ref, act_ref)
    h1_ref[...] = h1
    un = _rmsnorm(h1, gmix_ref[...]).astype(_BF16)
    z = _dot(un, win_ref[...]) + bin_ref[...]
    v_ref[...] = z[:, :CONV_WIDTH] * jax.nn.sigmoid(z[:, CONV_WIDTH:2 * CONV_WIDTH])
    zs_ref[...] = z[:, 2 * CONV_WIDTH:2 * CONV_WIDTH + SSM_WIDTH]
    gate_ref[...] = jax.nn.sigmoid(z[:, 2 * CONV_WIDTH + SSM_WIDTH:])


def _ffn_inproj(x, g1, wgu, wd, gmix, win, b_in):
    rows = x.shape[0]
    tm = FFN_ROW_TILE
    row = lambda w: pl.BlockSpec((tm, w), lambda i: (i, 0))
    return pl.pallas_call(
        _ffn_inproj_kernel,
        grid=(rows // tm,),
        in_specs=[row(D_MODEL), _const_spec((1, D_MODEL)),
                  _const_spec(wgu.shape), _const_spec(wd.shape),
                  _const_spec((1, D_MODEL)), _const_spec(win.shape), _const_spec((1, IN_COLS))],
        out_specs=[row(D_MODEL), row(CONV_WIDTH), row(SSM_WIDTH), row(2 * D_MODEL)],
        out_shape=[jax.ShapeDtypeStruct((rows, D_MODEL), _F32),
                   jax.ShapeDtypeStruct((rows, CONV_WIDTH), _F32),
                   jax.ShapeDtypeStruct((rows, SSM_WIDTH), _F32),
                   jax.ShapeDtypeStruct((rows, 2 * D_MODEL), _F32)],
        scratch_shapes=[pltpu.VMEM((tm, D_FF), _BF16)],
        compiler_params=pltpu.CompilerParams(dimension_semantics=("arbitrary",),
                                             vmem_limit_bytes=VMEM_LIMIT),
        name="ffn_inproj",
    )(x, g1, wgu, wd, gmix, win, b_in)


def _piece_masks(shape):
    piece = lax.broadcasted_iota(jnp.int32, shape, 1) // SSM_GROUP
    return [piece == p for p in range(1, GROUPS_PER_BLOCK)]


def _diagonal_select(xs, masks):
    nb = GROUPS_PER_BLOCK
    out = []
    for k in range(nb):
        acc = xs[-k % nb]
        for p in range(1, nb):
            acc = jnp.where(masks[p - 1], xs[(p - k) % nb], acc)
        out.append(acc)
    return out


def _ssm_scan_kernel(z_ref, w_ref, mv_ref, coef_ref, y_ref, a_ref, sf_ref, sb_ref, yy_ref,
                     *, n_chunks):
    nb = GROUPS_PER_BLOCK
    halves = CHUNK // nb

    def relayout_in(i, carry):
        rr = RELAYOUT_ROWS_IN
        row0 = pl.multiple_of(i * (rr * CHUNK), rr * CHUNK)
        crow = pl.multiple_of(i * rr, rr)
        masks = _piece_masks((rr // 2, LANES))
        for half in range(halves):
            xs = [pltpu.bitcast(z_ref[pl.ds(row0 + half * nb + t, rr, stride=CHUNK), :].astype(_BF16),
                                jnp.uint32) for t in range(nb)]
            xs = [x if t == 0 else pltpu.roll(x, SSM_GROUP * t, 1) for t, x in enumerate(xs)]
            for g, x in enumerate(_diagonal_select(xs, masks)):
                a_ref[g, pl.ds(crow, rr), half * LANES:(half + 1) * LANES] = pltpu.bitcast(x, _BF16)
        return carry

    lax.fori_loop(0, n_chunks // RELAYOUT_ROWS_IN, relayout_in, 0)

    for g in range(nb):
        s = _dot(a_ref[g], w_ref[g])
        packed = pl.ds(g, n_chunks, stride=nb)
        for d, s_ref in enumerate((sf_ref, sb_ref)):
            own = s[:, d * LANES:(d + 1) * LANES]
            s_ref[packed, :] = own
            yy_ref[d, packed, :] = pltpu.roll(own, SSM_STATE, 1)

    n_blocks = n_chunks // SCAN_STEPS
    span = SCAN_STEPS * nb
    s_refs = (sf_ref, sb_ref)
    coefs = [(coef_ref[3 * d], coef_ref[3 * d + 1], coef_ref[3 * d + 2]) for d in range(2)]

    def rows(blk):
        return pl.ds(pl.multiple_of(blk * span, span), span)

    def load(d, blk):
        return s_refs[d][rows(blk), :], yy_ref[d, rows(blk), :]

    def advance(d, blk, state):
        v, w, s_all, sw_all = state
        a, b, bp = coefs[d]
        entering = [None] * SCAN_STEPS
        for k in (range(SCAN_STEPS) if d == 0 else range(SCAN_STEPS - 1, -1, -1)):
            entering[k] = v
            own = slice(k * nb, (k + 1) * nb)
            v, w = a * v + b * w + s_all[own], a * w + bp * v + sw_all[own]
        s_refs[d][rows(blk), :] = jnp.concatenate(entering, axis=0)
        return v, w

    def scan_body(i, carry):
        fwd, bwd = carry
        blk_f, blk_b = i, n_blocks - 1 - i
        next_f = load(0, jnp.minimum(blk_f + 1, n_blocks - 1))
        next_b = load(1, jnp.maximum(blk_b - 1, 0))
        return advance(0, blk_f, fwd) + next_f, advance(1, blk_b, bwd) + next_b

    zero = jnp.zeros((nb, LANES), _F32)
    lax.fori_loop(0, n_blocks, scan_body,
                  ((zero, zero) + load(0, 0), (zero, zero) + load(1, n_blocks - 1)))

    for g in range(nb):
        lhs = jnp.concatenate([a_ref[g],
                               sf_ref[pl.ds(g, n_chunks, stride=nb), :].astype(_BF16),
                               sb_ref[pl.ds(g, n_chunks, stride=nb), :].astype(_BF16)], axis=1)
        yg = _dot(lhs, mv_ref[g])
        for half in range(halves):
            yy_ref[half, g * n_chunks:(g + 1) * n_chunks, :] = yg[:, half * LANES:(half + 1) * LANES]

    def relayout_out(i, carry):
        rr = RELAYOUT_ROWS_OUT
        row0 = pl.multiple_of(i * (rr * CHUNK), rr * CHUNK)
        crow = pl.multiple_of(i * rr, rr)
        masks = _piece_masks((rr, LANES))
        for half in range(halves):
            ys = [yy_ref[half, pl.ds(g * n_chunks + crow, rr), :] for g in range(nb)]
            for t, y in enumerate(_diagonal_select(ys, masks)):
                if t:
                    y = pltpu.roll(y, LANES - SSM_GROUP * t, 1)
                y_ref[pl.ds(row0 + half * nb + t, rr, stride=CHUNK), :] = y
        return carry

    lax.fori_loop(0, n_chunks // RELAYOUT_ROWS_OUT, relayout_out, 0)


def _ssm_scan(zs, w, mv, coef):
    bsz, length, _ = zs.shape
    n_chunks = length // CHUNK
    nb = GROUPS_PER_BLOCK
    block_bytes = length * LANES * 4
    scratch_bytes = n_chunks * nb * (FLAT * 2 + 2 * LANES * 4 + FLAT * 4)
    table_bytes = nb * (FLAT * FLAT + 2 * FLAT * FLAT) * 2
    double = 4 * block_bytes + scratch_bytes + table_bytes <= SCAN_VMEM_BUDGET
    seq_mode = pl.Buffered(2 if double else 1)
    once = pl.Buffered(1)
    seq_spec = pl.BlockSpec((None, length, LANES), lambda j, b: (b, 0, j), pipeline_mode=seq_mode)
    return pl.pallas_call(
        functools.partial(_ssm_scan_kernel, n_chunks=n_chunks),
        grid=(SSM_GROUPS // nb, bsz),
        in_specs=[seq_spec,
                  pl.BlockSpec((nb, FLAT, FLAT), lambda j, b: (j, 0, 0), pipeline_mode=once),
                  pl.BlockSpec((nb, 2 * FLAT, FLAT), lambda j, b: (j, 0, 0), pipeline_mode=once),
                  pl.BlockSpec((None, 6, nb, LANES), lambda j, b: (j, 0, 0, 0), pipeline_mode=once)],
        out_specs=seq_spec,
        out_shape=jax.ShapeDtypeStruct(zs.shape, _F32),
        scratch_shapes=[pltpu.VMEM((nb, n_chunks, FLAT), _BF16),
                        pltpu.VMEM((n_chunks * nb, LANES), _F32),
                        pltpu.VMEM((n_chunks * nb, LANES), _F32),
                        pltpu.VMEM((2, n_chunks * nb, LANES), _F32)],
        compiler_params=pltpu.CompilerParams(dimension_semantics=("arbitrary", "arbitrary"),
                                             vmem_limit_bytes=VMEM_LIMIT),
        name="ssm_scan",
    )(zs, w, mv, coef)


def _exact_dot(a, b):
    return jnp.dot(a, b, preferred_element_type=_F32, precision=lax.Precision.HIGHEST)


def _lane_window(cols, start):
    first, r = divmod(start, LANES)
    if r == 0:
        return cols[first]
    lane = lax.broadcasted_iota(jnp.int32, cols[first].shape, 1)
    return jnp.where(lane < LANES - r, pltpu.roll(cols[first], LANES - r, 1),
                     pltpu.roll(cols[first + 1], LANES - r, 1))


def _ssm_tables_kernel(pa_ref, pb_ref, ba_ref, bb_ref, ct_ref, pt_ref, bt_ref,
                       tile_ref, sel_ref, w_ref, mv_ref):
    nb = GROUPS_PER_BLOCK
    shift = (pl.program_id(0) % nb) * SSM_GROUP

    def rotated_block(s):
        return (s // nb) * nb + (s % nb + pl.program_id(0) % nb) % nb

    def rotate_pieces(x):
        return jnp.concatenate([pltpu.roll(x[:, :LANES], shift, 1),
                                pltpu.roll(x[:, LANES:], shift, 1)], axis=1)

    def rows_of(s):
        return pl.ds(pl.multiple_of(rotated_block(s) * SSM_GROUP, SSM_GROUP), SSM_GROUP)

    lag_stacks = []
    for d in range(2):
        for s in range(CHUNK):
            k = CHUNK - 1 - s if d == 0 else s
            blk = pa_ref[d, k:k + 1, :] * ba_ref[d] + pb_ref[d, k:k + 1, :] * bb_ref[d]
            w_ref[rows_of(s), d * LANES:(d + 1) * LANES] = blk.astype(_BF16)

        c_re = _exact_dot(ct_ref[d, 0], tile_ref[...])
        c_im = _exact_dot(ct_ref[d, 1], tile_ref[...])

        def c_times_power(which):
            p_re = _exact_dot(pt_ref[d, 0], sel_ref[which])
            p_im = _exact_dot(pt_ref[d, 1], sel_ref[which])
            return c_re * p_re - c_im * p_im, c_re * p_im + c_im * p_re

        v_re, v_im = c_times_power(d)
        base = FLAT + d * LANES
        mv_ref[base:base + SSM_STATE, :] = rotate_pieces(v_re).astype(_BF16)
        mv_ref[base + SSM_STATE:base + LANES, :] = rotate_pieces(-v_im).astype(_BF16)
        k_re, k_im = c_times_power(2 + d)
        lag_stacks.append(_exact_dot(bt_ref[d, 0], k_re) - _exact_dot(bt_ref[d, 1], k_im))

    zero = jnp.zeros((SSM_GROUP, LANES), _F32)
    fwd = [zero, zero, lag_stacks[0][:, :LANES], lag_stacks[0][:, LANES:], zero]
    bwd = [lag_stacks[1][:, :LANES], lag_stacks[1][:, LANES:], zero, zero, zero]
    for s in range(CHUNK):
        f0, b0 = FLAT - SSM_GROUP * s, SSM_GROUP * (CHUNK - 1 - s)
        blk = jnp.concatenate([_lane_window(fwd, f0) + _lane_window(bwd, b0),
                               _lane_window(fwd, f0 + LANES) + _lane_window(bwd, b0 + LANES)], axis=1)
        mv_ref[rows_of(s), :] = rotate_pieces(blk).astype(_BF16)


def _table_constants():
    t = np.arange(FLAT) // SSM_GROUP
    o = np.arange(FLAT) % SSM_GROUP
    tile = (np.arange(SSM_GROUP)[:, None] == o[None, :])
    k = np.arange(LANES)[:, None]
    sel = np.stack([k == t + 1, k == CHUNK - t, k == t, k == CHUNK - 1 - t])
    return jnp.asarray(tile, _F32), jnp.asarray(sel, _F32)


def _ssm_tables(lam_re, lam_im, log_dt, b_re, b_im, c_re, c_im):
    G, N = SSM_GROUPS, SSM_STATE
    dt = jnp.exp(log_dt)[..., None]
    mag = jnp.exp(lam_re * dt)
    lr, li = mag * jnp.cos(lam_im * dt), mag * jnp.sin(lam_im * dt)
    p_re, p_im = lr[:, :, None], li[:, :, None]
    while p_re.shape[2] < CHUNK:
        top_re, top_im = p_re[:, :, -1:], p_im[:, :, -1:]
        p_re, p_im = (jnp.concatenate([p_re, p_re * top_re - p_im * top_im], axis=2),
                      jnp.concatenate([p_im, p_re * top_im + p_im * top_re], axis=2))
    p_re = jnp.concatenate([jnp.ones_like(lr)[:, :, None], p_re], axis=2)
    p_im = jnp.concatenate([jnp.zeros_like(li)[:, :, None], p_im], axis=2)
    den = lam_re * lam_re + lam_im * lam_im
    qr = ((lr - 1.0) * lam_re + li * lam_im) / den
    qi = (li * lam_re - (lr - 1.0) * lam_im) / den
    bt_re = (qr[..., None] * b_re - qi[..., None] * b_im).transpose(0, 1, 3, 2)
    bt_im = (qr[..., None] * b_im + qi[..., None] * b_re).transpose(0, 1, 3, 2)
    pad_k = lambda x: jnp.pad(x, ((0, 0), (0, 0), (0, POWER_ROWS - CHUNK - 1), (0, 0)))
    per_group = lambda x: jnp.moveaxis(x, 1, 0)
    pa = per_group(pad_k(jnp.concatenate([p_re, p_im], -1)))
    pb = per_group(pad_k(jnp.concatenate([-p_im, p_re], -1)))
    ba = per_group(jnp.concatenate([bt_re, bt_re], -1))
    bb = per_group(jnp.concatenate([bt_im, bt_im], -1))
    ct = per_group(jnp.stack([c_re, c_im], axis=2).transpose(0, 1, 2, 4, 3))
    pt = jnp.stack([p_re, p_im], axis=2).transpose(0, 1, 2, 4, 3)
    pt = per_group(jnp.pad(pt, ((0, 0),) * 4 + ((0, LANES - CHUNK - 1),)))
    bt = per_group(jnp.stack([bt_re, bt_im], axis=2))
    tile, sel = _table_constants()
    group = lambda *tail: pl.BlockSpec((None,) + tail, lambda g: (g,) + (0,) * len(tail))
    const = lambda a: pl.BlockSpec(a.shape, lambda g: (0,) * a.ndim, pipeline_mode=pl.Buffered(1))
    w, mv = pl.pallas_call(
        _ssm_tables_kernel,
        grid=(G,),
        in_specs=[group(2, POWER_ROWS, LANES), group(2, POWER_ROWS, LANES),
                  group(2, SSM_GROUP, LANES), group(2, SSM_GROUP, LANES),
                  group(2, 2, N, SSM_GROUP), group(2, 2, N, LANES), group(2, 2, SSM_GROUP, N),
                  const(tile), const(sel)],
        out_specs=[group(FLAT, FLAT), group(2 * FLAT, FLAT)],
        out_shape=[jax.ShapeDtypeStruct((G, FLAT, FLAT), _BF16),
                   jax.ShapeDtypeStruct((G, 2 * FLAT, FLAT), _BF16)],
        compiler_params=pltpu.CompilerParams(dimension_semantics=("arbitrary",)),
        name="ssm_tables",
    )(pa, pb, ba, bb, ct, pt, bt, tile, sel)
    fr, fi = p_re[:, :, CHUNK], p_im[:, :, CHUNK]
    coef = jnp.stack([jnp.concatenate([fr, fr], -1), jnp.concatenate([-fi, fi], -1),
                      jnp.concatenate([fi, -fi], -1)], axis=1)
    coef = coef.reshape(6, G // GROUPS_PER_BLOCK, GROUPS_PER_BLOCK, 2 * N).transpose(1, 0, 2, 3)
    return w, mv, coef


def _depthwise_conv(buf_ref, cw_ref, c_ref, tm):
    n_tiles = CONV_ROWS // SUBLANES
    sublane = lax.broadcasted_iota(jnp.int32, (SUBLANES, CONV_COLS), 0)

    def rows(i, carry):
        r0 = pl.multiple_of(i * CONV_ROWS, CONV_ROWS)
        for c0 in range(0, CONV_WIDTH, CONV_COLS):
            cols = slice(c0, c0 + CONV_COLS)
            out = [None] * n_tiles
            for r in range(SUBLANES):
                part = None
                for m in range(HALO * 2 // SUBLANES):
                    k = SUBLANES * m + r - 1
                    if 0 <= k < CONV_K:
                        term = (buf_ref[pl.ds(r0 + SUBLANES * m, CONV_ROWS + SUBLANES), cols]
                                * cw_ref[k:k + 1, cols])
                        part = term if part is None else part + term
                tiles = [part[SUBLANES * j:SUBLANES * (j + 1)] for j in range(n_tiles + 1)]
                for j in range(n_tiles):
                    if r == 0:
                        shifted = tiles[j]
                    else:
                        shifted = pltpu.roll(jnp.where(sublane < r, tiles[j + 1], tiles[j]),
                                             SUBLANES - r, 0)
                    out[j] = shifted if out[j] is None else out[j] + shifted
            c_ref[pl.ds(r0, CONV_ROWS), cols] = jnp.concatenate(out, axis=0)
        return carry

    lax.fori_loop(0, tm // CONV_ROWS, rows, 0)


def _mix_kernel(v_ref, vp_ref, vn_ref, zs_ref, ys_ref, gate_ref, h1_ref,
                cw_ref, cb_ref, lng_ref, lnb_ref, wpw_ref, d_ref, wglu_ref, wout_ref,
                h2_ref, buf_ref, c_ref, *, tm):
    i = pl.program_id(1)
    last = pl.num_programs(1) - 1
    buf_ref[0:HALO, :] = jnp.where(i > 0, vp_ref[...], 0.0)
    buf_ref[HALO:HALO + tm, :] = v_ref[...]
    buf_ref[HALO + tm:2 * HALO + tm, :] = jnp.where(i < last, vn_ref[...], 0.0)
    _depthwise_conv(buf_ref, cw_ref, c_ref, tm)
    c = c_ref[...] + cb_ref[...]
    c = c - jnp.mean(c, axis=-1, keepdims=True)
    c = c * lax.rsqrt(jnp.mean(c * c, axis=-1, keepdims=True) + EPS) * lng_ref[...] + lnb_ref[...]
    conv_out = _dot((c * jax.nn.sigmoid(c)).astype(_BF16), wpw_ref[...])

    y = ys_ref[...] + d_ref[...] * zs_ref[...]
    ag = _dot(jax.nn.gelu(y).astype(_BF16), wglu_ref[...])
    ssm_out = ag[:, :D_MODEL] * jax.nn.sigmoid(ag[:, D_MODEL:])

    gate = gate_ref[...]
    merged = gate[:, :D_MODEL] * conv_out + gate[:, D_MODEL:] * ssm_out
    h2_ref[...] = h1_ref[...] + _dot(merged.astype(_BF16), wout_ref[...])


def _mix(v, zs, ys, gate, h1, cw, cb, lng, lnb, wpw, d, wglu, wout):
    bsz, length, _ = v.shape
    tm = ROW_TILE
    per = tm // HALO
    n_halo = length // HALO
    seq = lambda w: pl.BlockSpec((None, tm, w), lambda b, i: (b, i, 0))
    prev = pl.BlockSpec((None, HALO, CONV_WIDTH), lambda b, i: (b, jnp.maximum(i * per - 1, 0), 0))
    nxt = pl.BlockSpec((None, HALO, CONV_WIDTH),
                       lambda b, i: (b, jnp.minimum((i + 1) * per, n_halo - 1), 0))
    return pl.pallas_call(
        functools.partial(_mix_kernel, tm=tm),
        grid=(bsz, length // tm),
        in_specs=[seq(CONV_WIDTH), prev, nxt, seq(SSM_WIDTH), seq(SSM_WIDTH), seq(2 * D_MODEL),
                  seq(D_MODEL),
                  _const_spec(cw.shape), _const_spec((1, CONV_WIDTH)), _const_spec((1, CONV_WIDTH)),
                  _const_spec((1, CONV_WIDTH)), _const_spec(wpw.shape), _const_spec((1, SSM_WIDTH)),
                  _const_spec(wglu.shape), _const_spec(wout.shape)],
        out_specs=seq(D_MODEL),
        out_shape=jax.ShapeDtypeStruct((bsz, length, D_MODEL), _F32),
        scratch_shapes=[pltpu.VMEM((tm + 2 * HALO, CONV_WIDTH), _F32),
                        pltpu.VMEM((tm, CONV_WIDTH), _F32)],
        compiler_params=pltpu.CompilerParams(dimension_semantics=("arbitrary", "arbitrary"),
                                             vmem_limit_bytes=VMEM_LIMIT),
        name="mix",
    )(v, v, v, zs, ys, gate, h1, cw, cb, lng, lnb, wpw, d, wglu, wout)


def _mem_kv_kernel(mem_ref, g_ref, wkt_ref, wv_ref, kt_ref, v_ref):
    m = _rmsnorm(mem_ref[...], g_ref[...]).astype(_BF16)
    kt = lax.dot_general(wkt_ref[...], m, (((1,), (1,)), ((), ())), preferred_element_type=_F32)
    kt_ref[...] = kt.astype(_BF16)
    v_ref[...] = _dot(m, wv_ref[...]).astype(_BF16)


def _mem_kv(mem, g, wkt, wv):
    bsz = mem.shape[0]
    return pl.pallas_call(
        _mem_kv_kernel,
        grid=(bsz,),
        in_specs=[pl.BlockSpec((None, N_MEM, D_MODEL), lambda b: (b, 0, 0)),
                  _const_spec((1, D_MODEL)), _const_spec(wkt.shape), _const_spec(wv.shape)],
        out_specs=[pl.BlockSpec((None, D_MODEL, N_MEM), lambda b: (b, 0, 0)),
                   pl.BlockSpec((None, N_MEM, D_MODEL), lambda b: (b, 0, 0))],
        out_shape=[jax.ShapeDtypeStruct((bsz, D_MODEL, N_MEM), _BF16),
                   jax.ShapeDtypeStruct((bsz, N_MEM, D_MODEL), _BF16)],
        compiler_params=pltpu.CompilerParams(dimension_semantics=("arbitrary",)),
        name="mem_kv",
    )(mem, g, wkt, wv)


def _attn_ffn_kernel(h_ref, kt_ref, v_ref, gx_ref, wq_ref, wo_ref, g2_ref, wgu_ref, wd_ref,
                     gf_ref, out_ref, act_ref, o_ref):
    h = h_ref[...]
    q = _dot(_rmsnorm(h, gx_ref[...]).astype(_BF16), wq_ref[...]).astype(_BF16)
    for hd in range(X_HEADS):
        sl = slice(hd * X_HEAD_DIM, (hd + 1) * X_HEAD_DIM)
        s = _dot(q[:, sl], kt_ref[sl, :]) * (X_HEAD_DIM ** -0.5)
        e = jnp.exp(s - jnp.max(s, axis=-1, keepdims=True))
        p = e / jnp.sum(e, axis=-1, keepdims=True)
        o_ref[:, sl] = _dot(p.astype(_BF16), v_ref[:, sl]).astype(_BF16)
    h = h + _dot(o_ref[...], wo_ref[...])
    xn = _rmsnorm(h, g2_ref[...]).astype(_BF16)
    h = h + 0.5 * _swiglu(xn, wgu_ref, wd_ref, act_ref)
    out_ref[...] = _rmsnorm(h, gf_ref[...])


def _attn_ffn(h, kt, v, gx, wq, wo, g2, wgu, wd, gf):
    bsz, length, _ = h.shape
    tm = ATTN_ROW_TILE
    seq = pl.BlockSpec((None, tm, D_MODEL), lambda b, i: (b, i, 0))
    return pl.pallas_call(
        _attn_ffn_kernel,
        grid=(bsz, length // tm),
        in_specs=[seq,
                  pl.BlockSpec((None, D_MODEL, N_MEM), lambda b, i: (b, 0, 0)),
                  pl.BlockSpec((None, N_MEM, D_MODEL), lambda b, i: (b, 0, 0)),
                  _const_spec((1, D_MODEL)), _const_spec(wq.shape), _const_spec(wo.shape),
                  _const_spec((1, D_MODEL)), _const_spec(wgu.shape), _const_spec(wd.shape),
                  _const_spec((1, D_MODEL))],
        out_specs=seq,
        out_shape=jax.ShapeDtypeStruct(h.shape, _F32),
        scratch_shapes=[pltpu.VMEM((tm, D_FF), _BF16), pltpu.VMEM((tm, D_MODEL), _BF16)],
        compiler_params=pltpu.CompilerParams(dimension_semantics=("arbitrary", "arbitrary"),
                                             vmem_limit_bytes=VMEM_LIMIT),
        name="attn_ffn",
    )(h, kt, v, gx, wq, wo, g2, wgu, wd, gf)


def _row(vec):
    return vec.reshape(1, -1).astype(_F32)


def _trunk(x, mem, p):
    bsz, length, _ = x.shape
    h1, v, zs, gate = _ffn_inproj(x.reshape(bsz * length, D_MODEL), p['ffn1_g'], p['ffn1_wgu'],
                                  p['ffn1_wd'], p['mix_g'], p['w_in'], p['b_in'])
    seq = lambda a: a.reshape(bsz, length, a.shape[-1])
    h1, v, zs, gate = seq(h1), seq(v), seq(zs), seq(gate)
    ys = _ssm_scan(zs, p['ssm_w'], p['ssm_mv'], p['ssm_coef'])
    h2 = _mix(v, zs, ys, gate, h1, p['conv_w'], p['conv_b'], p['conv_ln_g'], p['conv_ln_b'],
              p['conv_w_pw'], p['ssm_d'], p['ssm_w_glu'], p['w_out'])
    kt, vm = _mem_kv(mem, p['mem_g'], p['xattn_wkt'], p['xattn_wv'])
    return _attn_ffn(h2, kt, vm, p['xattn_g'], p['xattn_wq'], p['xattn_wo'], p['ffn2_g'],
                     p['ffn2_wgu'], p['ffn2_wd'], p['final_g'])


def kernel(x_prompt, x_sample, mem_prompt, mem_sample, ffn1_g, ffn1_wgu, ffn1_wd, mix_g, w_in, b_in, conv_w, conv_b, conv_ln_g, conv_ln_b, conv_w_pw, ssm_lam_re, ssm_lam_im, ssm_log_dt, ssm_b_re, ssm_b_im, ssm_c_re, ssm_c_im, ssm_d, ssm_w_glu, w_out, xattn_g, mem_g, xattn_wq, xattn_wkv, xattn_wo, ffn2_g, ffn2_wgu, ffn2_wd, final_g):
    assert ffn1_g.shape[0] == 1, "single-layer trunk"
    p = {}
    p['ffn1_g'], p['mix_g'], p['xattn_g'] = _row(ffn1_g[0]), _row(mix_g[0]), _row(xattn_g[0])
    p['mem_g'], p['ffn2_g'], p['final_g'] = _row(mem_g[0]), _row(ffn2_g[0]), _row(final_g)
    p['ffn1_wgu'], p['ffn2_wgu'] = ffn1_wgu[0].astype(_BF16), ffn2_wgu[0].astype(_BF16)
    p['ffn1_wd'], p['ffn2_wd'] = ffn1_wd[0].astype(_BF16), ffn2_wd[0].astype(_BF16)
    p['w_in'], p['b_in'] = w_in[0].astype(_BF16), _row(b_in[0])
    p['conv_w'], p['conv_b'] = conv_w[0].astype(_F32), _row(conv_b[0])
    p['conv_ln_g'], p['conv_ln_b'] = _row(conv_ln_g[0]), _row(conv_ln_b[0])
    p['conv_w_pw'] = conv_w_pw[0].astype(_BF16)
    p['ssm_w'], p['ssm_mv'], p['ssm_coef'] = _ssm_tables(
        ssm_lam_re[0], ssm_lam_im[0], ssm_log_dt[0], ssm_b_re[0], ssm_b_im[0],
        ssm_c_re[0], ssm_c_im[0])
    p['ssm_d'], p['ssm_w_glu'] = _row(ssm_d[0]), ssm_w_glu[0].astype(_BF16)
    p['w_out'] = w_out[0].astype(_BF16)
    p['xattn_wq'], p['xattn_wo'] = xattn_wq[0].astype(_BF16), xattn_wo[0].astype(_BF16)
    p['xattn_wkt'] = xattn_wkv[0][:, :D_MODEL].T.astype(_BF16)
    p['xattn_wv'] = xattn_wkv[0][:, D_MODEL:].astype(_BF16)
    return (_trunk(x_prompt, mem_prompt, p), _trunk(x_sample, mem_sample, p))
```

```python
import functools

import jax
import jax.numpy as jnp
import numpy as np
from jax import lax
from jax.experimental import pallas as pl
from jax.experimental.pallas import tpu as pltpu

D_MODEL = 1024
D_FF = 2816
FF_CHUNK = 256
N_FF_CHUNKS = D_FF // FF_CHUNK
CONV_WIDTH = 512
CONV_K = 31
CONV_PAD = CONV_K // 2
SUBLANES = 8
LANES = 128
HALO = 2 * SUBLANES
CONV_ROWS = 128
CONV_COLS = 128
SSM_WIDTH = 512
SSM_GROUP = 16
SSM_GROUPS = SSM_WIDTH // SSM_GROUP
SSM_STATE = 64
CHUNK = 16
FLAT = CHUNK * SSM_GROUP
GROUPS_PER_BLOCK = LANES // SSM_GROUP
RELAYOUT_ROWS_IN = 64
RELAYOUT_ROWS_OUT = 64
SCAN_STEPS = 8
POWER_ROWS = 24
N_MEM = 256
X_HEADS = 4
X_HEAD_DIM = D_MODEL // X_HEADS
EPS = 1e-6
IN_COLS = 2 * CONV_WIDTH + SSM_WIDTH + 2 * D_MODEL

ROW_TILE = 512
FFN_ROW_TILE = 512
ATTN_ROW_TILE = 1024
VMEM_LIMIT = 60 * 1024 * 1024
SCAN_VMEM_BUDGET = 44 * 1024 * 1024

_F32 = jnp.float32
_BF16 = jnp.bfloat16


def _dot(a, b):
    return jnp.dot(a, b, preferred_element_type=_F32)


def _rmsnorm(x, g):
    return x * lax.rsqrt(jnp.mean(x * x, axis=-1, keepdims=True) + EPS) * g


def _const_spec(shape):
    nd = len(shape)
    return pl.BlockSpec(shape, lambda *_: (0,) * nd, pipeline_mode=pl.Buffered(1))


def _swiglu(xn, wgu_ref, wd_ref, act_ref):
    for j in range(N_FF_CHUNKS):
        lo = j * FF_CHUNK
        g = _dot(xn, wgu_ref[:, lo:lo + FF_CHUNK])
        u = _dot(xn, wgu_ref[:, D_FF + lo:D_FF + lo + FF_CHUNK])
        act_ref[:, lo:lo + FF_CHUNK] = (g * jax.nn.sigmoid(g) * u).astype(_BF16)
    return _dot(act_ref[...], wd_ref[...])


def _ffn_inproj_kernel(x_ref, g1_ref, wgu_ref, wd_ref, gmix_ref, win_ref, bin_ref,
                       h1_ref, v_ref, zs_ref, gate_ref, act_ref):
    x = x_ref[...]
    xn = _rmsnorm(x, g1_ref[...]).astype(_BF16)
    h1 = x + 0.5 * _swiglu(xn, wgu_ref, wd_ref, act_ref)
    h1_ref[...] = h1
    un = _rmsnorm(h1, gmix_ref[...]).astype(_BF16)
    z = _dot(un, win_ref[...]) + bin_ref[...]
    v_ref[...] = z[:, :CONV_WIDTH] * jax.nn.sigmoid(z[:, CONV_WIDTH:2 * CONV_WIDTH])
    zs_ref[...] = z[:, 2 * CONV_WIDTH:2 * CONV_WIDTH + SSM_WIDTH]
    gate_ref[...] = jax.nn.sigmoid(z[:, 2 * CONV_WIDTH + SSM_WIDTH:])


def _ffn_inproj(x, g1, wgu, wd, gmix, win, b_in):
    rows = x.shape[0]
    tm = FFN_ROW_TILE
    row = lambda w: pl.BlockSpec((tm, w), lambda i: (i, 0))
    return pl.pallas_call(
        _ffn_inproj_kernel,
        grid=(rows // tm,),
        in_specs=[row(D_MODEL), _const_spec((1, D_MODEL)),
                  _const_spec(wgu.shape), _const_spec(wd.shape),
                  _const_spec((1, D_MODEL)), _const_spec(win.shape), _const_spec((1, IN_COLS))],
        out_specs=[row(D_MODEL), row(CONV_WIDTH), row(SSM_WIDTH), row(2 * D_MODEL)],
        out_shape=[jax.ShapeDtypeStruct((rows, D_MODEL), _F32),
                   jax.ShapeDtypeStruct((rows, CONV_WIDTH), _F32),
                   jax.ShapeDtypeStruct((rows, SSM_WIDTH), _F32),
                   jax.ShapeDtypeStruct((rows, 2 * D_MODEL), _F32)],
        scratch_shapes=[pltpu.VMEM((tm, D_FF), _BF16)],
        compiler_params=pltpu.CompilerParams(dimension_semantics=("arbitrary",),
                                             vmem_limit_bytes=VMEM_LIMIT),
        name="ffn_inproj",
    )(x, g1, wgu, wd, gmix, win, b_in)


def _piece_masks(shape):
    piece = lax.broadcasted_iota(jnp.int32, shape, 1) // SSM_GROUP
    return [piece == p for p in range(1, GROUPS_PER_BLOCK)]


def _diagonal_select(xs, masks):
    nb = GROUPS_PER_BLOCK
    out = []
    for k in range(nb):
        acc = xs[-k % nb]
        for p in range(1, nb):
            acc = jnp.where(masks[p - 1], xs[(p - k) % nb], acc)
        out.append(acc)
    return out


def _ssm_scan_kernel(z_ref, w_ref, mv_ref, coef_ref, y_ref, a_ref, sf_ref, sb_ref, yy_ref,
                     *, n_chunks):
    nb = GROUPS_PER_BLOCK
    halves = CHUNK // nb

    def relayout_in(i, carry):
        rr = RELAYOUT_ROWS_IN
        row0 = pl.multiple_of(i * (rr * CHUNK), rr * CHUNK)
        crow = pl.multiple_of(i * rr, rr)
        masks = _piece_masks((rr // 2, LANES))
        for half in range(halves):
            xs = [pltpu.bitcast(z_ref[pl.ds(row0 + half * nb + t, rr, stride=CHUNK), :].astype(_BF16),
                                jnp.uint32) for t in range(nb)]
            xs = [x if t == 0 else pltpu.roll(x, SSM_GROUP * t, 1) for t, x in enumerate(xs)]
            for g, x in enumerate(_diagonal_select(xs, masks)):
                a_ref[g, pl.ds(crow, rr), half * LANES:(half + 1) * LANES] = pltpu.bitcast(x, _BF16)
        return carry

    lax.fori_loop(0, n_chunks // RELAYOUT_ROWS_IN, relayout_in, 0)

    for g in range(nb):
        s = _dot(a_ref[g], w_ref[g])
        packed = pl.ds(g, n_chunks, stride=nb)
        for d, s_ref in enumerate((sf_ref, sb_ref)):
            own = s[:, d * LANES:(d + 1) * LANES]
            s_ref[packed, :] = own
            yy_ref[d, packed, :] = pltpu.roll(own, SSM_STATE, 1)

    n_blocks = n_chunks // SCAN_STEPS
    span = SCAN_STEPS * nb
    s_refs = (sf_ref, sb_ref)
    coefs = [(coef_ref[3 * d], coef_ref[3 * d + 1], coef_ref[3 * d + 2]) for d in range(2)]

    def rows(blk):
        return pl.ds(pl.multiple_of(blk * span, span), span)

    def load(d, blk):
        return s_refs[d][rows(blk), :], yy_ref[d, rows(blk), :]

    def advance(d, blk, state):
        v, w, s_all, sw_all = state
        a, b, bp = coefs[d]
        entering = [None] * SCAN_STEPS
        for k in (range(SCAN_STEPS) if d == 0 else range(SCAN_STEPS - 1, -1, -1)):
            entering[k] = v
            own = slice(k * nb, (k + 1) * nb)
            v, w = a * v + b * w + s_all[own], a * w + bp * v + sw_all[own]
        s_refs[d][rows(blk), :] = jnp.concatenate(entering, axis=0)
        return v, w

    def scan_body(i, carry):
        fwd, bwd = carry
        blk_f, blk_b = i, n_blocks - 1 - i
        next_f = load(0, jnp.minimum(blk_f + 1, n_blocks - 1))
        next_b = load(1, jnp.maximum(blk_b - 1, 0))
        return advance(0, blk_f, fwd) + next_f, advance(1, blk_b, bwd) + next_b

    zero = jnp.zeros((nb, LANES), _F32)
    lax.fori_loop(0, n_blocks, scan_body,
                  ((zero, zero) + load(0, 0), (zero, zero) + load(1, n_blocks - 1)))

    for g in range(nb):
        lhs = jnp.concatenate([a_ref[g],
                               sf_ref[pl.ds(g, n_chunks, stride=nb), :].astype(_BF16),
                               sb_ref[pl.ds(g, n_chunks, stride=nb), :].astype(_BF16)], axis=1)
        yg = _dot(lhs, mv_ref[g])
        for half in range(halves):
            yy_ref[half, g * n_chunks:(g + 1) * n_chunks, :] = yg[:, half * LANES:(half + 1) * LANES]

    def relayout_out(i, carry):
        rr = RELAYOUT_ROWS_OUT
        row0 = pl.multiple_of(i * (rr * CHUNK), rr * CHUNK)
        crow = pl.multiple_of(i * rr, rr)
        masks = _piece_masks((rr, LANES))
        for half in range(halves):
            ys = [yy_ref[half, pl.ds(g * n_chunks + crow, rr), :] for g in range(nb)]
            for t, y in enumerate(_diagonal_select(ys, masks)):
                if t:
                    y = pltpu.roll(y, LANES - SSM_GROUP * t, 1)
                y_ref[pl.ds(row0 + half * nb + t, rr, stride=CHUNK), :] = y
        return carry

    lax.fori_loop(0, n_chunks // RELAYOUT_ROWS_OUT, relayout_out, 0)


def _ssm_scan(zs, w, mv, coef):
    bsz, length, _ = zs.shape
    n_chunks = length // CHUNK
    nb = GROUPS_PER_BLOCK
    block_bytes = length * LANES * 4
    scratch_bytes = n_chunks * nb * (FLAT * 2 + 2 * LANES * 4 + FLAT * 4)
    table_bytes = nb * (FLAT * FLAT + 2 * FLAT * FLAT) * 2
    double = 4 * block_bytes + scratch_bytes + table_bytes <= SCAN_VMEM_BUDGET
    seq_mode = pl.Buffered(2 if double else 1)
    once = pl.Buffered(1)
    seq_spec = pl.BlockSpec((None, length, LANES), lambda j, b: (b, 0, j), pipeline_mode=seq_mode)
    return pl.pallas_call(
        functools.partial(_ssm_scan_kernel, n_chunks=n_chunks),
        grid=(SSM_GROUPS // nb, bsz),
        in_specs=[seq_spec,
                  pl.BlockSpec((nb, FLAT, FLAT), lambda j, b: (j, 0, 0), pipeline_mode=once),
                  pl.BlockSpec((nb, 2 * FLAT, FLAT), lambda j, b: (j, 0, 0), pipeline_mode=once),
                  pl.BlockSpec((None, 6, nb, LANES), lambda j, b: (j, 0, 0, 0), pipeline_mode=once)],
        out_specs=seq_spec,
        out_shape=jax.ShapeDtypeStruct(zs.shape, _F32),
        scratch_shapes=[pltpu.VMEM((nb, n_chunks, FLAT), _BF16),
                        pltpu.VMEM((n_chunks * nb, LANES), _F32),
                        pltpu.VMEM((n_chunks * nb, LANES), _F32),
                        pltpu.VMEM((2, n_chunks * nb, LANES), _F32)],
        compiler_params=pltpu.CompilerParams(dimension_semantics=("arbitrary", "arbitrary"),
                                             vmem_limit_bytes=VMEM_LIMIT),
        name="ssm_scan",
    )(zs, w, mv, coef)


def _exact_dot(a, b):
    return jnp.dot(a, b, preferred_element_type=_F32, precision=lax.Precision.HIGHEST)


def _lane_window(cols, start):
    first, r = divmod(start, LANES)
    if r == 0:
        return cols[first]
    lane = lax.broadcasted_iota(jnp.int32, cols[first].shape, 1)
    return jnp.where(lane < LANES - r, pltpu.roll(cols[first], LANES - r, 1),
                     pltpu.roll(cols[first + 1], LANES - r, 1))


def _ssm_tables_kernel(pa_ref, pb_ref, ba_ref, bb_ref, ct_ref, pt_ref, bt_ref,
                       tile_ref, sel_ref, w_ref, mv_ref):
    nb = GROUPS_PER_BLOCK
    shift = (pl.program_id(0) % nb) * SSM_GROUP

    def rotated_block(s):
        return (s // nb) * nb + (s % nb + pl.program_id(0) % nb) % nb

    def rotate_pieces(x):
        return jnp.concatenate([pltpu.roll(x[:, :LANES], shift, 1),
                                pltpu.roll(x[:, LANES:], shift, 1)], axis=1)

    def rows_of(s):
        return pl.ds(pl.multiple_of(rotated_block(s) * SSM_GROUP, SSM_GROUP), SSM_GROUP)

    lag_stacks = []
    for d in range(2):
        for s in range(CHUNK):
            k = CHUNK - 1 - s if d == 0 else s
            blk = pa_ref[d, k:k + 1, :] * ba_ref[d] + pb_ref[d, k:k + 1, :] * bb_ref[d]
            w_ref[rows_of(s), d * LANES:(d + 1) * LANES] = blk.astype(_BF16)

        c_re = _exact_dot(ct_ref[d, 0], tile_ref[...])
        c_im = _exact_dot(ct_ref[d, 1], tile_ref[...])

        def c_times_power(which):
            p_re = _exact_dot(pt_ref[d, 0], sel_ref[which])
            p_im = _exact_dot(pt_ref[d, 1], sel_ref[which])
            return c_re * p_re - c_im * p_im, c_re * p_im + c_im * p_re

        v_re, v_im = c_times_power(d)
        base = FLAT + d * LANES
        mv_ref[base:base + SSM_STATE, :] = rotate_pieces(v_re).astype(_BF16)
        mv_ref[base + SSM_STATE:base + LANES, :] = rotate_pieces(-v_im).astype(_BF16)
        k_re, k_im = c_times_power(2 + d)
        lag_stacks.append(_exact_dot(bt_ref[d, 0], k_re) - _exact_dot(bt_ref[d, 1], k_im))

    zero = jnp.zeros((SSM_GROUP, LANES), _F32)
    fwd = [zero, zero, lag_stacks[0][:, :LANES], lag_stacks[0][:, LANES:], zero]
    bwd = [lag_stacks[1][:, :LANES], lag_stacks[1][:, LANES:], zero, zero, zero]
    for s in range(CHUNK):
        f0, b0 = FLAT - SSM_GROUP * s, SSM_GROUP * (CHUNK - 1 - s)
        blk = jnp.concatenate([_lane_window(fwd, f0) + _lane_window(bwd, b0),
                               _lane_window(fwd, f0 + LANES) + _lane_window(bwd, b0 + LANES)], axis=1)
        mv_ref[rows_of(s), :] = rotate_pieces(blk).astype(_BF16)


def _table_constants():
    t = np.arange(FLAT) // SSM_GROUP
    o = np.arange(FLAT) % SSM_GROUP
    tile = (np.arange(SSM_GROUP)[:, None] == o[None, :])
    k = np.arange(LANES)[:, None]
    sel = np.stack([k == t + 1, k == CHUNK - t, k == t, k == CHUNK - 1 - t])
    return jnp.asarray(tile, _F32), jnp.asarray(sel, _F32)


def _ssm_tables(lam_re, lam_im, log_dt, b_re, b_im, c_re, c_im):
    G, N = SSM_GROUPS, SSM_STATE
    dt = jnp.exp(log_dt)[..., None]
    mag = jnp.exp(lam_re * dt)
    lr, li = mag * jnp.cos(lam_im * dt), mag * jnp.sin(lam_im * dt)
    p_re, p_im = lr[:, :, None], li[:, :, None]
    while p_re.shape[2] < CHUNK:
        top_re, top_im = p_re[:, :, -1:], p_im[:, :, -1:]
        p_re, p_im = (jnp.concatenate([p_re, p_re * top_re - p_im * top_im], axis=2),
                      jnp.concatenate([p_im, p_re * top_im + p_im * top_re], axis=2))
    p_re = jnp.concatenate([jnp.ones_like(lr)[:, :, None], p_re], axis=2)
    p_im = jnp.concatenate([jnp.zeros_like(li)[:, :, None], p_im], axis=2)
    den = lam_re * lam_re + lam_im * lam_im
    qr = ((lr - 1.0) * lam_re + li * lam_im) / den
    qi = (li * lam_re - (lr - 1.0) * lam_im) / den
    bt_re = (qr[..., None] * b_re - qi[..., None] * b_im).transpose(0, 1, 3, 2)
    bt_im = (qr[..., None] * b_im + qi[..., None] * b_re).transpose(0, 1, 3, 2)
    pad_k = lambda x: jnp.pad(x, ((0, 0), (0, 0), (0, POWER_ROWS - CHUNK - 1), (0, 0)))
    per_group = lambda x: jnp.moveaxis(x, 1, 0)
    pa = per_group(pad_k(jnp.concatenate([p_re, p_im], -1)))
    pb = per_group(pad_k(jnp.concatenate([-p_im, p_re], -1)))
    ba = per_group(jnp.concatenate([bt_re, bt_re], -1))
    bb = per_group(jnp.concatenate([bt_im, bt_im], -1))
    ct = per_group(jnp.stack([c_re, c_im], axis=2).transpose(0, 1, 2, 4, 3))
    pt = jnp.stack([p_re, p_im], axis=2).transpose(0, 1, 2, 4, 3)
    pt = per_group(jnp.pad(pt, ((0, 0),) * 4 + ((0, LANES - CHUNK - 1),)))
    bt = per_group(jnp.stack([bt_re, bt_im], axis=2))
    tile, sel = _table_constants()
    group = lambda *tail: pl.BlockSpec((None,) + tail, lambda g: (g,) + (0,) * len(tail))
    const = lambda a: pl.BlockSpec(a.shape, lambda g: (0,) * a.ndim, pipeline_mode=pl.Buffered(1))
    w, mv = pl.pallas_call(
        _ssm_tables_kernel,
        grid=(G,),
        in_specs=[group(2, POWER_ROWS, LANES), group(2, POWER_ROWS, LANES),
                  group(2, SSM_GROUP, LANES), group(2, SSM_GROUP, LANES),
                  group(2, 2, N, SSM_GROUP), group(2, 2, N, LANES), group(2, 2, SSM_GROUP, N),
                  const(tile), const(sel)],
        out_specs=[group(FLAT, FLAT), group(2 * FLAT, FLAT)],
        out_shape=[jax.ShapeDtypeStruct((G, FLAT, FLAT), _BF16),
                   jax.ShapeDtypeStruct((G, 2 * FLAT, FLAT), _BF16)],
        compiler_params=pltpu.CompilerParams(dimension_semantics=("arbitrary",)),
        name="ssm_tables",
    )(pa, pb, ba, bb, ct, pt, bt, tile, sel)
    fr, fi = p_re[:, :, CHUNK], p_im[:, :, CHUNK]
    coef = jnp.stack([jnp.concatenate([fr, fr], -1), jnp.concatenate([-fi, fi], -1),
                      jnp.concatenate([fi, -fi], -1)], axis=1)
    coef = coef.reshape(6, G // GROUPS_PER_BLOCK, GROUPS_PER_BLOCK, 2 * N).transpose(1, 0, 2, 3)
    return w, mv, coef


def _depthwise_conv(buf_ref, cw_ref, c_ref, tm):
    n_tiles = CONV_ROWS // SUBLANES
    sublane = lax.broadcasted_iota(jnp.int32, (SUBLANES, CONV_COLS), 0)

    def rows(i, carry):
        r0 = pl.multiple_of(i * CONV_ROWS, CONV_ROWS)
        for c0 in range(0, CONV_WIDTH, CONV_COLS):
            cols = slice(c0, c0 + CONV_COLS)
            out = [None] * n_tiles
            for r in range(SUBLANES):
                part = None
                for m in range(HALO * 2 // SUBLANES):
                    k = SUBLANES * m + r - 1
                    if 0 <= k < CONV_K:
                        term = (buf_ref[pl.ds(r0 + SUBLANES * m, CONV_ROWS + SUBLANES), cols]
                                * cw_ref[k:k + 1, cols])
                        part = term if part is None else part + term
                tiles = [part[SUBLANES * j:SUBLANES * (j + 1)] for j in range(n_tiles + 1)]
                for j in range(n_tiles):
                    if r == 0:
                        shifted = tiles[j]
                    else:
                        shifted = pltpu.roll(jnp.where(sublane < r, tiles[j + 1], tiles[j]),
                                             SUBLANES - r, 0)
                    out[j] = shifted if out[j] is None else out[j] + shifted
            c_ref[pl.ds(r0, CONV_ROWS), cols] = jnp.concatenate(out, axis=0)
        return carry

    lax.fori_loop(0, tm // CONV_ROWS, rows, 0)


def _mix_kernel(v_ref, vp_ref, vn_ref, zs_ref, ys_ref, gate_ref, h1_ref,
                cw_ref, cb_ref, lng_ref, lnb_ref, wpw_ref, d_ref, wglu_ref, wout_ref,
                h2_ref, buf_ref, c_ref, *, tm):
    i = pl.program_id(1)
    last = pl.num_programs(1) - 1
    buf_ref[0:HALO, :] = jnp.where(i > 0, vp_ref[...], 0.0)
    buf_ref[HALO:HALO + tm, :] = v_ref[...]
    buf_ref[HALO + tm:2 * HALO + tm, :] = jnp.where(i < last, vn_ref[...], 0.0)
    _depthwise_conv(buf_ref, cw_ref, c_ref, tm)
    c = c_ref[...] + cb_ref[...]
    c = c - jnp.mean(c, axis=-1, keepdims=True)
    c = c * lax.rsqrt(jnp.mean(c * c, axis=-1, keepdims=True) + EPS) * lng_ref[...] + lnb_ref[...]
    conv_out = _dot((c * jax.nn.sigmoid(c)).astype(_BF16), wpw_ref[...])

    y = ys_ref[...] + d_ref[...] * zs_ref[...]
    ag = _dot(jax.nn.gelu(y).astype(_BF16), wglu_ref[...])
    ssm_out = ag[:, :D_MODEL] * jax.nn.sigmoid(ag[:, D_MODEL:])

    gate = gate_ref[...]
    merged = gate[:, :D_MODEL] * conv_out + gate[:, D_MODEL:] * ssm_out
    h2_ref[...] = h1_ref[...] + _dot(merged.astype(_BF16), wout_ref[...])


def _mix(v, zs, ys, gate, h1, cw, cb, lng, lnb, wpw, d, wglu, wout):
    bsz, length, _ = v.shape
    tm = ROW_TILE
    per = tm // HALO
    n_halo = length // HALO
    seq = lambda w: pl.BlockSpec((None, tm, w), lambda b, i: (b, i, 0))
    prev = pl.BlockSpec((None, HALO, CONV_WIDTH), lambda b, i: (b, jnp.maximum(i * per - 1, 0), 0))
    nxt = pl.BlockSpec((None, HALO, CONV_WIDTH),
                       lambda b, i: (b, jnp.minimum((i + 1) * per, n_halo - 1), 0))
    return pl.pallas_call(
        functools.partial(_mix_kernel, tm=tm),
        grid=(bsz, length // tm),
        in_specs=[seq(CONV_WIDTH), prev, nxt, seq(SSM_WIDTH), seq(SSM_WIDTH), seq(2 * D_MODEL),
                  seq(D_MODEL),
                  _const_spec(cw.shape), _const_spec((1, CONV_WIDTH)), _const_spec((1, CONV_WIDTH)),
                  _const_spec((1, CONV_WIDTH)), _const_spec(wpw.shape), _const_spec((1, SSM_WIDTH)),
                  _const_spec(wglu.shape), _const_spec(wout.shape)],
        out_specs=seq(D_MODEL),
        out_shape=jax.ShapeDtypeStruct((bsz, length, D_MODEL), _F32),
        scratch_shapes=[pltpu.VMEM((tm + 2 * HALO, CONV_WIDTH), _F32),
                        pltpu.VMEM((tm, CONV_WIDTH), _F32)],
        compiler_params=pltpu.CompilerParams(dimension_semantics=("arbitrary", "arbitrary"),
                                             vmem_limit_bytes=VMEM_LIMIT),
        name="mix",
    )(v, v, v, zs, ys, gate, h1, cw, cb, lng, lnb, wpw, d, wglu, wout)


def _mem_kv_kernel(mem_ref, g_ref, wkt_ref, wv_ref, kt_ref, v_ref):
    m = _rmsnorm(mem_ref[...], g_ref[...]).astype(_BF16)
    kt = lax.dot_general(wkt_ref[...], m, (((1,), (1,)), ((), ())), preferred_element_type=_F32)
    kt_ref[...] = kt.astype(_BF16)
    v_ref[...] = _dot(m, wv_ref[...]).astype(_BF16)


def _mem_kv(mem, g, wkt, wv):
    bsz = mem.shape[0]
    return pl.pallas_call(
        _mem_kv_kernel,
        grid=(bsz,),
        in_specs=[pl.BlockSpec((None, N_MEM, D_MODEL), lambda b: (b, 0, 0)),
                  _const_spec((1, D_MODEL)), _const_spec(wkt.shape), _const_spec(wv.shape)],
        out_specs=[pl.BlockSpec((None, D_MODEL, N_MEM), lambda b: (b, 0, 0)),
                   pl.BlockSpec((None, N_MEM, D_MODEL), lambda b: (b, 0, 0))],
        out_shape=[jax.ShapeDtypeStruct((bsz, D_MODEL, N_MEM), _BF16),
                   jax.ShapeDtypeStruct((bsz, N_MEM, D_MODEL), _BF16)],
        compiler_params=pltpu.CompilerParams(dimension_semantics=("arbitrary",)),
        name="mem_kv",
    )(mem, g, wkt, wv)


def _attn_ffn_kernel(h_ref, kt_ref, v_ref, gx_ref, wq_ref, wo_ref, g2_ref, wgu_ref, wd_ref,
                     gf_ref, out_ref, act_ref, o_ref):
    h = h_ref[...]
    q = _dot(_rmsnorm(h, gx_ref[...]).astype(_BF16), wq_ref[...]).astype(_BF16)
    for hd in range(X_HEADS):
        sl = slice(hd * X_HEAD_DIM, (hd + 1) * X_HEAD_DIM)
        s = _dot(q[:, sl], kt_ref[sl, :]) * (X_HEAD_DIM ** -0.5)
        e = jnp.exp(s - jnp.max(s, axis=-1, keepdims=True))
        p = e / jnp.sum(e, axis=-1, keepdims=True)
        o_ref[:, sl] = _dot(p.astype(_BF16), v_ref[:, sl]).astype(_BF16)
    h = h + _dot(o_ref[...], wo_ref[...])
    xn = _rmsnorm(h, g2_ref[...]).astype(_BF16)
    h = h + 0.5 * _swiglu(xn, wgu_ref, wd_ref, act_ref)
    out_ref[...] = _rmsnorm(h, gf_ref[...])


def _attn_ffn(h, kt, v, gx, wq, wo, g2, wgu, wd, gf):
    bsz, length, _ = h.shape
    tm = ATTN_ROW_TILE
    seq = pl.BlockSpec((None, tm, D_MODEL), lambda b, i: (b, i, 0))
    return pl.pallas_call(
        _attn_ffn_kernel,
        grid=(bsz, length // tm),
        in_specs=[seq,
                  pl.BlockSpec((None, D_MODEL, N_MEM), lambda b, i: (b, 0, 0)),
                  pl.BlockSpec((None, N_MEM, D_MODEL), lambda b, i: (b, 0, 0)),
                  _const_spec((1, D_MODEL)), _const_spec(wq.shape), _const_spec(wo.shape),
                  _const_spec((1, D_MODEL)), _const_spec(wgu.shape), _const_spec(wd.shape),
                  _const_spec((1, D_MODEL))],
        out_specs=seq,
        out_shape=jax.ShapeDtypeStruct(h.shape, _F32),
        scratch_shapes=[pltpu.VMEM((tm, D_FF), _BF16), pltpu.VMEM((tm, D_MODEL), _BF16)],
        compiler_params=pltpu.CompilerParams(dimension_semantics=("arbitrary", "arbitrary"),
                                             vmem_limit_bytes=VMEM_LIMIT),
        name="attn_ffn",
    )(h, kt, v, gx, wq, wo, g2, wgu, wd, gf)


def _row(vec):
    return vec.reshape(1, -1).astype(_F32)


def _trunk(x, mem, p):
    bsz, length, _ = x.shape
    h1, v, zs, gate = _ffn_inproj(x.reshape(bsz * length, D_MODEL), p['ffn1_g'], p['ffn1_wgu'],
                                  p['ffn1_wd'], p['mix_g'], p['w_in'], p['b_in'])
    seq = lambda a: a.reshape(bsz, length, a.shape[-1])
    h1, v, zs, gate = seq(h1), seq(v), seq(zs), seq(gate)
    ys = _ssm_scan(zs, p['ssm_w'], p['ssm_mv'], p['ssm_coef'])
    h2 = _mix(v, zs, ys, gate, h1, p['conv_w'], p['conv_b'], p['conv_ln_g'], p['conv_ln_b'],
              p['conv_w_pw'], p['ssm_d'], p['ssm_w_glu'], p['w_out'])
    kt, vm = _mem_kv(mem, p['mem_g'], p['xattn_wkt'], p['xattn_wv'])
    return _attn_ffn(h2, kt, vm, p['xattn_g'], p['xattn_wq'], p['xattn_wo'], p['ffn2_g'],
                     p['ffn2_wgu'], p['ffn2_wd'], p['final_g'])


def kernel(x_prompt, x_sample, mem_prompt, mem_sample, ffn1_g, ffn1_wgu, ffn1_wd, mix_g, w_in, b_in, conv_w, conv_b, conv_ln_g, conv_ln_b, conv_w_pw, ssm_lam_re, ssm_lam_im, ssm_log_dt, ssm_b_re, ssm_b_im, ssm_c_re, ssm_c_im, ssm_d, ssm_w_glu, w_out, xattn_g, mem_g, xattn_wq, xattn_wkv, xattn_wo, ffn2_g, ffn2_wgu, ffn2_wd, final_g):
    assert ffn1_g.shape[0] == 1, "single-layer trunk"
    p = {}
    p['ffn1_g'], p['mix_g'], p['xattn_g'] = _row(ffn1_g[0]), _row(mix_g[0]), _row(xattn_g[0])
    p['mem_g'], p['ffn2_g'], p['final_g'] = _row(mem_g[0]), _row(ffn2_g[0]), _row(final_g)
    p['ffn1_wgu'], p['ffn2_wgu'] = ffn1_wgu[0].astype(_BF16), ffn2_wgu[0].astype(_BF16)
    p['ffn1_wd'], p['ffn2_wd'] = ffn1_wd[0].astype(_BF16), ffn2_wd[0].astype(_BF16)
    p['w_in'], p['b_in'] = w_in[0].astype(_BF16), _row(b_in[0])
    p['conv_w'], p['conv_b'] = conv_w[0].astype(_F32), _row(conv_b[0])
    p['conv_ln_g'], p['conv_ln_b'] = _row(conv_ln_g[0]), _row(conv_ln_b[0])
    p['conv_w_pw'] = conv_w_pw[0].astype(_BF16)
    p['ssm_w'], p['ssm_mv'], p['ssm_coef'] = _ssm_tables(
        ssm_lam_re[0], ssm_lam_im[0], ssm_log_dt[0], ssm_b_re[0], ssm_b_im[0],
        ssm_c_re[0], ssm_c_im[0])
    p['ssm_d'], p['ssm_w_glu'] = _row(ssm_d[0]), ssm_w_glu[0].astype(_BF16)
    p['w_out'] = w_out[0].astype(_BF16)
    p['xattn_wq'], p['xattn_wo'] = xattn_wq[0].astype(_BF16), xattn_wo[0].astype(_BF16)
    p['xattn_wkt'] = xattn_wkv[0][:, :D_MODEL].T.astype(_BF16)
    p['xattn_wv'] = xattn_wkv[0][:, D_MODEL:].astype(_BF16)
    return (_trunk(x_prompt, mem_prompt, p), _trunk(x_sample, mem_sample, p))
```

```python
import functools

import jax
import jax.numpy as jnp
import numpy as np
from jax import lax
from jax.experimental import pallas as pl
from jax.experimental.pallas import tpu as pltpu

D_MODEL = 1024
D_FF = 2816
FF_CHUNK = 256
N_FF_CHUNKS = D_FF // FF_CHUNK
CONV_WIDTH = 512
CONV_K = 31
CONV_PAD = CONV_K // 2
SUBLANES = 8
LANES = 128
HALO = 2 * SUBLANES
CONV_ROWS = 256
CONV_COLS = 128
SSM_WIDTH = 512
SSM_GROUP = 16
SSM_GROUPS = SSM_WIDTH // SSM_GROUP
SSM_STATE = 64
CHUNK = 16
FLAT = CHUNK * SSM_GROUP
GROUPS_PER_BLOCK = LANES // SSM_GROUP
RELAYOUT_ROWS_IN = 64
RELAYOUT_ROWS_OUT = 64
SCAN_STEPS = 8
POWER_ROWS = 24
N_MEM = 256
X_HEADS = 4
X_HEAD_DIM = D_MODEL // X_HEADS
EPS = 1e-6
IN_COLS = 2 * CONV_WIDTH + SSM_WIDTH + 2 * D_MODEL

ROW_TILE = 512
FFN_ROW_TILE = 512
ATTN_ROW_TILE = 1024
VMEM_LIMIT = 60 * 1024 * 1024
SCAN_VMEM_BUDGET = 44 * 1024 * 1024

_F32 = jnp.float32
_BF16 = jnp.bfloat16


def _dot(a, b):
    return jnp.dot(a, b, preferred_element_type=_F32)


def _rmsnorm(x, g):
    return x * lax.rsqrt(jnp.mean(x * x, axis=-1, keepdims=True) + EPS) * g


def _const_spec(shape):
    nd = len(shape)
    return pl.BlockSpec(shape, lambda *_: (0,) * nd, pipeline_mode=pl.Buffered(1))


def _swiglu(xn, wgu_ref, wd_ref, act_ref):
    for j in range(N_FF_CHUNKS):
        lo = j * FF_CHUNK
        g = _dot(xn, wgu_ref[:, lo:lo + FF_CHUNK])
        u = _dot(xn, wgu_ref[:, D_FF + lo:D_FF + lo + FF_CHUNK])
        act_ref[:, lo:lo + FF_CHUNK] = (g * jax.nn.sigmoid(g) * u).astype(_BF16)
    return _dot(act_ref[...], wd_ref[...])


def _ffn_inproj_kernel(x_ref, g1_ref, wgu_ref, wd_ref, gmix_ref, win_ref, bin_ref,
                       h1_ref, v_ref, zs_ref, gate_ref, act_ref):
    x = x_ref[...]
    xn = _rmsnorm(x, g1_ref[...]).astype(_BF16)
    h1 = x + 0.5 * _swiglu(xn, wgu_ref, wd_ref, act_ref)
    h1_ref[...] = h1
    un = _rmsnorm(h1, gmix_ref[...]).astype(_BF16)
    z = _dot(un, win_ref[...]) + bin_ref[...]
    v_ref[...] = z[:, :CONV_WIDTH] * jax.nn.sigmoid(z[:, CONV_WIDTH:2 * CONV_WIDTH])
    zs_ref[...] = z[:, 2 * CONV_WIDTH:2 * CONV_WIDTH + SSM_WIDTH]
    gate_ref[...] = jax.nn.sigmoid(z[:, 2 * CONV_WIDTH + SSM_WIDTH:])


def _ffn_inproj(x, g1, wgu, wd, gmix, win, b_in):
    rows = x.shape[0]
    tm = FFN_ROW_TILE
    row = lambda w: pl.BlockSpec((tm, w), lambda i: (i, 0))
    return pl.pallas_call(
        _ffn_inproj_kernel,
        grid=(rows // tm,),
        in_specs=[row(D_MODEL), _const_spec((1, D_MODEL)),
                  _const_spec(wgu.shape), _const_spec(wd.shape),
                  _const_spec((1, D_MODEL)), _const_spec(win.shape), _const_spec((1, IN_COLS))],
        out_specs=[row(D_MODEL), row(CONV_WIDTH), row(SSM_WIDTH), row(2 * D_MODEL)],
        out_shape=[jax.ShapeDtypeStruct((rows, D_MODEL), _F32),
                   jax.ShapeDtypeStruct((rows, CONV_WIDTH), _F32),
                   jax.ShapeDtypeStruct((rows, SSM_WIDTH), _F32),
                   jax.ShapeDtypeStruct((rows, 2 * D_MODEL), _F32)],
        scratch_shapes=[pltpu.VMEM((tm, D_FF), _BF16)],
        compiler_params=pltpu.CompilerParams(dimension_semantics=("arbitrary",),
                                             vmem_limit_bytes=VMEM_LIMIT),
        name="ffn_inproj",
    )(x, g1, wgu, wd, gmix, win, b_in)


def _piece_masks(shape):
    piece = lax.broadcasted_iota(jnp.int32, shape, 1) // SSM_GROUP
    return [piece == p for p in range(1, GROUPS_PER_BLOCK)]


def _diagonal_select(xs, masks):
    nb = GROUPS_PER_BLOCK
    out = []
    for k in range(nb):
        acc = xs[-k % nb]
        for p in range(1, nb):
            acc = jnp.where(masks[p - 1], xs[(p - k) % nb], acc)
        out.append(acc)
    return out


def _ssm_scan_kernel(z_ref, w_ref, mv_ref, coef_ref, y_ref, a_ref, sf_ref, sb_ref, yy_ref,
                     *, n_chunks):
    nb = GROUPS_PER_BLOCK
    halves = CHUNK // nb

    def relayout_in(i, carry):
        rr = RELAYOUT_ROWS_IN
        row0 = pl.multiple_of(i * (rr * CHUNK), rr * CHUNK)
        crow = pl.multiple_of(i * rr, rr)
        masks = _piece_masks((rr // 2, LANES))
        for half in range(halves):
            xs = [pltpu.bitcast(z_ref[pl.ds(row0 + half * nb + t, rr, stride=CHUNK), :].astype(_BF16),
                                jnp.uint32) for t in range(nb)]
            xs = [x if t == 0 else pltpu.roll(x, SSM_GROUP * t, 1) for t, x in enumerate(xs)]
            for g, x in enumerate(_diagonal_select(xs, masks)):
                a_ref[g, pl.ds(crow, rr), half * LANES:(half + 1) * LANES] = pltpu.bitcast(x, _BF16)
        return carry

    lax.fori_loop(0, n_chunks // RELAYOUT_ROWS_IN, relayout_in, 0)

    for g in range(nb):
        s = _dot(a_ref[g], w_ref[g])
        packed = pl.ds(g, n_chunks, stride=nb)
        for d, s_ref in enumerate((sf_ref, sb_ref)):
            own = s[:, d * LANES:(d + 1) * LANES]
            s_ref[packed, :] = own
            yy_ref[d, packed, :] = pltpu.roll(own, SSM_STATE, 1)

    n_blocks = n_chunks // SCAN_STEPS
    span = SCAN_STEPS * nb
    s_refs = (sf_ref, sb_ref)
    coefs = [(coef_ref[3 * d], coef_ref[3 * d + 1], coef_ref[3 * d + 2]) for d in range(2)]

    def rows(blk):
        return pl.ds(pl.multiple_of(blk * span, span), span)

    def load(d, blk):
        return s_refs[d][rows(blk), :], yy_ref[d, rows(blk), :]

    def advance(d, blk, state):
        v, w, s_all, sw_all = state
        a, b, bp = coefs[d]
        entering = [None] * SCAN_STEPS
        for k in (range(SCAN_STEPS) if d == 0 else range(SCAN_STEPS - 1, -1, -1)):
            entering[k] = v
            own = slice(k * nb, (k + 1) * nb)
            v, w = a * v + b * w + s_all[own], a * w + bp * v + sw_all[own]
        s_refs[d][rows(blk), :] = jnp.concatenate(entering, axis=0)
        return v, w

    def scan_body(i, carry):
        fwd, bwd = carry
        blk_f, blk_b = i, n_blocks - 1 - i
        next_f = load(0, jnp.minimum(blk_f + 1, n_blocks - 1))
        next_b = load(1, jnp.maximum(blk_b - 1, 0))
        return advance(0, blk_f, fwd) + next_f, advance(1, blk_b, bwd) + next_b

    zero = jnp.zeros((nb, LANES), _F32)
    lax.fori_loop(0, n_blocks, scan_body,
                  ((zero, zero) + load(0, 0), (zero, zero) + load(1, n_blocks - 1)))

    for g in range(nb):
        lhs = jnp.concatenate([a_ref[g],
                               sf_ref[pl.ds(g, n_chunks, stride=nb), :].astype(_BF16),
                               sb_ref[pl.ds(g, n_chunks, stride=nb), :].astype(_BF16)], axis=1)
        yg = _dot(lhs, mv_ref[g])
        for half in range(halves):
            yy_ref[half, g * n_chunks:(g + 1) * n_chunks, :] = yg[:, half * LANES:(half + 1) * LANES]

    def relayout_out(i, carry):
        rr = RELAYOUT_ROWS_OUT
        row0 = pl.multiple_of(i * (rr * CHUNK), rr * CHUNK)
        crow = pl.multiple_of(i * rr, rr)
        masks = _piece_masks((rr, LANES))
        for half in range(halves):
            ys = [yy_ref[half, pl.ds(g * n_chunks + crow, rr), :] for g in range(nb)]
            for t, y in enumerate(_diagonal_select(ys, masks)):
                if t:
                    y = pltpu.roll(y, LANES - SSM_GROUP * t, 1)
                y_ref[pl.ds(row0 + half * nb + t, rr, stride=CHUNK), :] = y
        return carry

    lax.fori_loop(0, n_chunks // RELAYOUT_ROWS_OUT, relayout_out, 0)


def _ssm_scan(zs, w, mv, coef):
    bsz, length, _ = zs.shape
    n_chunks = length // CHUNK
    nb = GROUPS_PER_BLOCK
    block_bytes = length * LANES * 4
    scratch_bytes = n_chunks * nb * (FLAT * 2 + 2 * LANES * 4 + FLAT * 4)
    table_bytes = nb * (FLAT * FLAT + 2 * FLAT * FLAT) * 2
    double = 4 * block_bytes + scratch_bytes + table_bytes <= SCAN_VMEM_BUDGET
    seq_mode = pl.Buffered(2 if double else 1)
    once = pl.Buffered(1)
    seq_spec = pl.BlockSpec((None, length, LANES), lambda j, b: (b, 0, j), pipeline_mode=seq_mode)
    return pl.pallas_call(
        functools.partial(_ssm_scan_kernel, n_chunks=n_chunks),
        grid=(SSM_GROUPS // nb, bsz),
        in_specs=[seq_spec,
                  pl.BlockSpec((nb, FLAT, FLAT), lambda j, b: (j, 0, 0), pipeline_mode=once),
                  pl.BlockSpec((nb, 2 * FLAT, FLAT), lambda j, b: (j, 0, 0), pipeline_mode=once),
                  pl.BlockSpec((None, 6, nb, LANES), lambda j, b: (j, 0, 0, 0), pipeline_mode=once)],
        out_specs=seq_spec,
        out_shape=jax.ShapeDtypeStruct(zs.shape, _F32),
        scratch_shapes=[pltpu.VMEM((nb, n_chunks, FLAT), _BF16),
                        pltpu.VMEM((n_chunks * nb, LANES), _F32),
                        pltpu.VMEM((n_chunks * nb, LANES), _F32),
                        pltpu.VMEM((2, n_chunks * nb, LANES), _F32)],
        compiler_params=pltpu.CompilerParams(dimension_semantics=("arbitrary", "arbitrary"),
                                             vmem_limit_bytes=VMEM_LIMIT),
        name="ssm_scan",
    )(zs, w, mv, coef)


def _exact_dot(a, b):
    return jnp.dot(a, b, preferred_element_type=_F32, precision=lax.Precision.HIGHEST)


def _lane_window(cols, start):
    first, r = divmod(start, LANES)
    if r == 0:
        return cols[first]
    lane = lax.broadcasted_iota(jnp.int32, cols[first].shape, 1)
    return jnp.where(lane < LANES - r, pltpu.roll(cols[first], LANES - r, 1),
                     pltpu.roll(cols[first + 1], LANES - r, 1))


def _ssm_tables_kernel(pa_ref, pb_ref, ba_ref, bb_ref, ct_ref, pt_ref, bt_ref,
                       tile_ref, sel_ref, w_ref, mv_ref):
    nb = GROUPS_PER_BLOCK
    shift = (pl.program_id(0) % nb) * SSM_GROUP

    def rotated_block(s):
        return (s // nb) * nb + (s % nb + pl.program_id(0) % nb) % nb

    def rotate_pieces(x):
        return jnp.concatenate([pltpu.roll(x[:, :LANES], shift, 1),
                                pltpu.roll(x[:, LANES:], shift, 1)], axis=1)

    def rows_of(s):
        return pl.ds(pl.multiple_of(rotated_block(s) * SSM_GROUP, SSM_GROUP), SSM_GROUP)

    lag_stacks = []
    for d in range(2):
        for s in range(CHUNK):
            k = CHUNK - 1 - s if d == 0 else s
            blk = pa_ref[d, k:k + 1, :] * ba_ref[d] + pb_ref[d, k:k + 1, :] * bb_ref[d]
            w_ref[rows_of(s), d * LANES:(d + 1) * LANES] = blk.astype(_BF16)

        c_re = _exact_dot(ct_ref[d, 0], tile_ref[...])
        c_im = _exact_dot(ct_ref[d, 1], tile_ref[...])

        def c_times_power(which):
            p_re = _exact_dot(pt_ref[d, 0], sel_ref[which])
            p_im = _exact_dot(pt_ref[d, 1], sel_ref[which])
            return c_re * p_re - c_im * p_im, c_re * p_im + c_im * p_re

        v_re, v_im = c_times_power(d)
        base = FLAT + d * LANES
        mv_ref[base:base + SSM_STATE, :] = rotate_pieces(v_re).astype(_BF16)
        mv_ref[base + SSM_STATE:base + LANES, :] = rotate_pieces(-v_im).astype(_BF16)
        k_re, k_im = c_times_power(2 + d)
        lag_stacks.append(_exact_dot(bt_ref[d, 0], k_re) - _exact_dot(bt_ref[d, 1], k_im))

    zero = jnp.zeros((SSM_GROUP, LANES), _F32)
    fwd = [zero, zero, lag_stacks[0][:, :LANES], lag_stacks[0][:, LANES:], zero]
    bwd = [lag_stacks[1][:, :LANES], lag_stacks[1][:, LANES:], zero, zero, zero]
    for s in range(CHUNK):
        f0, b0 = FLAT - SSM_GROUP * s, SSM_GROUP * (CHUNK - 1 - s)
        blk = jnp.concatenate([_lane_window(fwd, f0) + _lane_window(bwd, b0),
                               _lane_window(fwd, f0 + LANES) + _lane_window(bwd, b0 + LANES)], axis=1)
        mv_ref[rows_of(s), :] = rotate_pieces(blk).astype(_BF16)


def _table_constants():
    t = np.arange(FLAT) // SSM_GROUP
    o = np.arange(FLAT) % SSM_GROUP
    tile = (np.arange(SSM_GROUP)[:, None] == o[None, :])
    k = np.arange(LANES)[:, None]
    sel = np.stack([k == t + 1, k == CHUNK - t, k == t, k == CHUNK - 1 - t])
    return jnp.asarray(tile, _F32), jnp.asarray(sel, _F32)


def _ssm_tables(lam_re, lam_im, log_dt, b_re, b_im, c_re, c_im):
    G, N = SSM_GROUPS, SSM_STATE
    dt = jnp.exp(log_dt)[..., None]
    mag = jnp.exp(lam_re * dt)
    lr, li = mag * jnp.cos(lam_im * dt), mag * jnp.sin(lam_im * dt)
    p_re, p_im = lr[:, :, None], li[:, :, None]
    while p_re.shape[2] < CHUNK:
        top_re, top_im = p_re[:, :, -1:], p_im[:, :, -1:]
        p_re, p_im = (jnp.concatenate([p_re, p_re * top_re - p_im * top_im], axis=2),
                      jnp.concatenate([p_im, p_re * top_im + p_im * top_re], axis=2))
    p_re = jnp.concatenate([jnp.ones_like(lr)[:, :, None], p_re], axis=2)
    p_im = jnp.concatenate([jnp.zeros_like(li)[:, :, None], p_im], axis=2)
    den = lam_re * lam_re + lam_im * lam_im
    qr = ((lr - 1.0) * lam_re + li * lam_im) / den
    qi = (li * lam_re - (lr - 1.0) * lam_im) / den
    bt_re = (qr[..., None] * b_re - qi[..., None] * b_im).transpose(0, 1, 3, 2)
    bt_im = (qr[..., None] * b_im + qi[..., None] * b_re).transpose(0, 1, 3, 2)
    pad_k = lambda x: jnp.pad(x, ((0, 0), (0, 0), (0, POWER_ROWS - CHUNK - 1), (0, 0)))
    per_group = lambda x: jnp.moveaxis(x, 1, 0)
    pa = per_group(pad_k(jnp.concatenate([p_re, p_im], -1)))
    pb = per_group(pad_k(jnp.concatenate([-p_im, p_re], -1)))
    ba = per_group(jnp.concatenate([bt_re, bt_re], -1))
    bb = per_group(jnp.concatenate([bt_im, bt_im], -1))
    ct = per_group(jnp.stack([c_re, c_im], axis=2).transpose(0, 1, 2, 4, 3))
    pt = jnp.stack([p_re, p_im], axis=2).transpose(0, 1, 2, 4, 3)
    pt = per_group(jnp.pad(pt, ((0, 0),) * 4 + ((0, LANES - CHUNK - 1),)))
    bt = per_group(jnp.stack([bt_re, bt_im], axis=2))
    tile, sel = _table_constants()
    group = lambda *tail: pl.BlockSpec((None,) + tail, lambda g: (g,) + (0,) * len(tail))
    const = lambda a: pl.BlockSpec(a.shape, lambda g: (0,) * a.ndim, pipeline_mode=pl.Buffered(1))
    w, mv = pl.pallas_call(
        _ssm_tables_kernel,
        grid=(G,),
        in_specs=[group(2, POWER_ROWS, LANES), group(2, POWER_ROWS, LANES),
                  group(2, SSM_GROUP, LANES), group(2, SSM_GROUP, LANES),
                  group(2, 2, N, SSM_GROUP), group(2, 2, N, LANES), group(2, 2, SSM_GROUP, N),
                  const(tile), const(sel)],
        out_specs=[group(FLAT, FLAT), group(2 * FLAT, FLAT)],
        out_shape=[jax.ShapeDtypeStruct((G, FLAT, FLAT), _BF16),
                   jax.ShapeDtypeStruct((G, 2 * FLAT, FLAT), _BF16)],
        compiler_params=pltpu.CompilerParams(dimension_semantics=("arbitrary",)),
        name="ssm_tables",
    )(pa, pb, ba, bb, ct, pt, bt, tile, sel)
    fr, fi = p_re[:, :, CHUNK], p_im[:, :, CHUNK]
    coef = jnp.stack([jnp.concatenate([fr, fr], -1), jnp.concatenate([-fi, fi], -1),
                      jnp.concatenate([fi, -fi], -1)], axis=1)
    coef = coef.reshape(6, G // GROUPS_PER_BLOCK, GROUPS_PER_BLOCK, 2 * N).transpose(1, 0, 2, 3)
    return w, mv, coef


def _depthwise_conv(buf_ref, cw_ref, c_ref, tm):
    n_tiles = CONV_ROWS // SUBLANES
    sublane = lax.broadcasted_iota(jnp.int32, (SUBLANES, CONV_COLS), 0)

    def rows(i, carry):
        r0 = pl.multiple_of(i * CONV_ROWS, CONV_ROWS)
        for c0 in range(0, CONV_WIDTH, CONV_COLS):
            cols = slice(c0, c0 + CONV_COLS)
            out = [None] * n_tiles
            for r in range(SUBLANES):
                part = None
                for m in range(HALO * 2 // SUBLANES):
                    k = SUBLANES * m + r - (HALO - CONV_PAD)
                    if 0 <= k < CONV_K:
                        term = (buf_ref[pl.ds(r0 + SUBLANES * m, CONV_ROWS + SUBLANES), cols]
                                * cw_ref[k:k + 1, cols])
                        part = term if part is None else part + term
                tiles = [part[SUBLANES * j:SUBLANES * (j + 1)] for j in range(n_tiles + 1)]
                for j in range(n_tiles):
                    if r == 0:
                        shifted = tiles[j]
                    else:
                        shifted = pltpu.roll(jnp.where(sublane < r, tiles[j + 1], tiles[j]),
                                             SUBLANES - r, 0)
                    out[j] = shifted if out[j] is None else out[j] + shifted
            c_ref[pl.ds(r0, CONV_ROWS), cols] = jnp.concatenate(out, axis=0)
        return carry

    lax.fori_loop(0, tm // CONV_ROWS, rows, 0)


def _mix_kernel(v_ref, vp_ref, vn_ref, zs_ref, ys_ref, gate_ref, h1_ref,
                cw_ref, cb_ref, lng_ref, lnb_ref, wpw_ref, d_ref, wglu_ref, wout_ref,
                h2_ref, buf_ref, c_ref, *, tm):
    i = pl.program_id(1)
    last = pl.num_programs(1) - 1
    buf_ref[0:HALO, :] = jnp.where(i > 0, vp_ref[...], 0.0)
    buf_ref[HALO:HALO + tm, :] = v_ref[...]
    buf_ref[HALO + tm:2 * HALO + tm, :] = jnp.where(i < last, vn_ref[...], 0.0)
    _depthwise_conv(buf_ref, cw_ref, c_ref, tm)
    c = c_ref[...] + cb_ref[...]
    c = c - jnp.mean(c, axis=-1, keepdims=True)
    c = c * lax.rsqrt(jnp.mean(c * c, axis=-1, keepdims=True) + EPS) * lng_ref[...] + lnb_ref[...]
    conv_out = _dot((c * jax.nn.sigmoid(c)).astype(_BF16), wpw_ref[...])

    y = ys_ref[...] + d_ref[...] * zs_ref[...]
    ag = _dot(jax.nn.gelu(y).astype(_BF16), wglu_ref[...])
    ssm_out = ag[:, :D_MODEL] * jax.nn.sigmoid(ag[:, D_MODEL:])

    gate = gate_ref[...]
    merged = gate[:, :D_MODEL] * conv_out + gate[:, D_MODEL:] * ssm_out
    h2_ref[...] = h1_ref[...] + _dot(merged.astype(_BF16), wout_ref[...])


def _mix(v, zs, ys, gate, h1, cw, cb, lng, lnb, wpw, d, wglu, wout):
    bsz, length, _ = v.shape
    tm = ROW_TILE
    per = tm // HALO
    n_halo = length // HALO
    seq = lambda w: pl.BlockSpec((None, tm, w), lambda b, i: (b, i, 0))
    prev = pl.BlockSpec((None, HALO, CONV_WIDTH), lambda b, i: (b, jnp.maximum(i * per - 1, 0), 0))
    nxt = pl.BlockSpec((None, HALO, CONV_WIDTH),
                       lambda b, i: (b, jnp.minimum((i + 1) * per, n_halo - 1), 0))
    return pl.pallas_call(
        functools.partial(_mix_kernel, tm=tm),
        grid=(bsz, length // tm),
        in_specs=[seq(CONV_WIDTH), prev, nxt, seq(SSM_WIDTH), seq(SSM_WIDTH), seq(2 * D_MODEL),
                  seq(D_MODEL),
                  _const_spec(cw.shape), _const_spec((1, CONV_WIDTH)), _const_spec((1, CONV_WIDTH)),
                  _const_spec((1, CONV_WIDTH)), _const_spec(wpw.shape), _const_spec((1, SSM_WIDTH)),
                  _const_spec(wglu.shape), _const_spec(wout.shape)],
        out_specs=seq(D_MODEL),
        out_shape=jax.ShapeDtypeStruct((bsz, length, D_MODEL), _F32),
        scratch_shapes=[pltpu.VMEM((tm + 2 * HALO, CONV_WIDTH), _F32),
                        pltpu.VMEM((tm, CONV_WIDTH), _F32)],
        compiler_params=pltpu.CompilerParams(dimension_semantics=("arbitrary", "arbitrary"),
                                             vmem_limit_bytes=VMEM_LIMIT),
        name="mix",
    )(v, v, v, zs, ys, gate, h1, cw, cb, lng, lnb, wpw, d, wglu, wout)


def _mem_kv_kernel(mem_ref, g_ref, wkt_ref, wv_ref, kt_ref, v_ref):
    m = _rmsnorm(mem_ref[...], g_ref[...]).astype(_BF16)
    kt = lax.dot_general(wkt_ref[...], m, (((1,), (1,)), ((), ())), preferred_element_type=_F32)
    kt_ref[...] = kt.astype(_BF16)
    v_ref[...] = _dot(m, wv_ref[...]).astype(_BF16)


def _mem_kv(mem, g, wkt, wv):
    bsz = mem.shape[0]
    return pl.pallas_call(
        _mem_kv_kernel,
        grid=(bsz,),
        in_specs=[pl.BlockSpec((None, N_MEM, D_MODEL), lambda b: (b, 0, 0)),
                  _const_spec((1, D_MODEL)), _const_spec(wkt.shape), _const_spec(wv.shape)],
        out_specs=[pl.BlockSpec((None, D_MODEL, N_MEM), lambda b: (b, 0, 0)),
                   pl.BlockSpec((None, N_MEM, D_MODEL), lambda b: (b, 0, 0))],
        out_shape=[jax.ShapeDtypeStruct((bsz, D_MODEL, N_MEM), _BF16),
                   jax.ShapeDtypeStruct((bsz, N_MEM, D_MODEL), _BF16)],
        compiler_params=pltpu.CompilerParams(dimension_semantics=("arbitrary",)),
        name="mem_kv",
    )(mem, g, wkt, wv)


def _attn_ffn_kernel(h_ref, kt_ref, v_ref, gx_ref, wq_ref, wo_ref, g2_ref, wgu_ref, wd_ref,
                     gf_ref, out_ref, act_ref, o_ref):
    h = h_ref[...]
    q = _dot(_rmsnorm(h, gx_ref[...]).astype(_BF16), wq_ref[...]).astype(_BF16)
    for hd in range(X_HEADS):
        sl = slice(hd * X_HEAD_DIM, (hd + 1) * X_HEAD_DIM)
        s = _dot(q[:, sl], kt_ref[sl, :]) * (X_HEAD_DIM ** -0.5)
        e = jnp.exp(s - jnp.max(s, axis=-1, keepdims=True))
        p = e / jnp.sum(e, axis=-1, keepdims=True)
        o_ref[:, sl] = _dot(p.astype(_BF16), v_ref[:, sl]).astype(_BF16)
    h = h + _dot(o_ref[...], wo_ref[...])
    xn = _rmsnorm(h, g2_ref[...]).astype(_BF16)
    h = h + 0.5 * _swiglu(xn, wgu_ref, wd_ref, act_ref)
    out_ref[...] = _rmsnorm(h, gf_ref[...])


def _attn_ffn(h, kt, v, gx, wq, wo, g2, wgu, wd, gf):
    bsz, length, _ = h.shape
    tm = ATTN_ROW_TILE
    seq = pl.BlockSpec((None, tm, D_MODEL), lambda b, i: (b, i, 0))
    return pl.pallas_call(
        _attn_ffn_kernel,
        grid=(bsz, length // tm),
        in_specs=[seq,
                  pl.BlockSpec((None, D_MODEL, N_MEM), lambda b, i: (b, 0, 0)),
                  pl.BlockSpec((None, N_MEM, D_MODEL), lambda b, i: (b, 0, 0)),
                  _const_spec((1, D_MODEL)), _const_spec(wq.shape), _const_spec(wo.shape),
                  _const_spec((1, D_MODEL)), _const_spec(wgu.shape), _const_spec(wd.shape),
                  _const_spec((1, D_MODEL))],
        out_specs=seq,
        out_shape=jax.ShapeDtypeStruct(h.shape, _F32),
        scratch_shapes=[pltpu.VMEM((tm, D_FF), _BF16), pltpu.VMEM((tm, D_MODEL), _BF16)],
        compiler_params=pltpu.CompilerParams(dimension_semantics=("arbitrary", "arbitrary"),
                                             vmem_limit_bytes=VMEM_LIMIT),
        name="attn_ffn",
    )(h, kt, v, gx, wq, wo, g2, wgu, wd, gf)


def _row(vec):
    return vec.reshape(1, -1).astype(_F32)


def _trunk(x, mem, p):
    bsz, length, _ = x.shape
    h1, v, zs, gate = _ffn_inproj(x.reshape(bsz * length, D_MODEL), p['ffn1_g'], p['ffn1_wgu'],
                                  p['ffn1_wd'], p['mix_g'], p['w_in'], p['b_in'])
    seq = lambda a: a.reshape(bsz, length, a.shape[-1])
    h1, v, zs, gate = seq(h1), seq(v), seq(zs), seq(gate)
    ys = _ssm_scan(zs, p['ssm_w'], p['ssm_mv'], p['ssm_coef'])
    h2 = _mix(v, zs, ys, gate, h1, p['conv_w'], p['conv_b'], p['conv_ln_g'], p['conv_ln_b'],
              p['conv_w_pw'], p['ssm_d'], p['ssm_w_glu'], p['w_out'])
    kt, vm = _mem_kv(mem, p['mem_g'], p['xattn_wkt'], p['xattn_wv'])
    return _attn_ffn(h2, kt, vm, p['xattn_g'], p['xattn_wq'], p['xattn_wo'], p['ffn2_g'],
                     p['ffn2_wgu'], p['ffn2_wd'], p['final_g'])


def kernel(x_prompt, x_sample, mem_prompt, mem_sample, ffn1_g, ffn1_wgu, ffn1_wd, mix_g, w_in, b_in, conv_w, conv_b, conv_ln_g, conv_ln_b, conv_w_pw, ssm_lam_re, ssm_lam_im, ssm_log_dt, ssm_b_re, ssm_b_im, ssm_c_re, ssm_c_im, ssm_d, ssm_w_glu, w_out, xattn_g, mem_g, xattn_wq, xattn_wkv, xattn_wo, ffn2_g, ffn2_wgu, ffn2_wd, final_g):
    assert ffn1_g.shape[0] == 1, "single-layer trunk"
    p = {}
    p['ffn1_g'], p['mix_g'], p['xattn_g'] = _row(ffn1_g[0]), _row(mix_g[0]), _row(xattn_g[0])
    p['mem_g'], p['ffn2_g'], p['final_g'] = _row(mem_g[0]), _row(ffn2_g[0]), _row(final_g)
    p['ffn1_wgu'], p['ffn2_wgu'] = ffn1_wgu[0].astype(_BF16), ffn2_wgu[0].astype(_BF16)
    p['ffn1_wd'], p['ffn2_wd'] = ffn1_wd[0].astype(_BF16), ffn2_wd[0].astype(_BF16)
    p['w_in'], p['b_in'] = w_in[0].astype(_BF16), _row(b_in[0])
    p['conv_w'], p['conv_b'] = conv_w[0].astype(_F32), _row(conv_b[0])
    p['conv_ln_g'], p['conv_ln_b'] = _row(conv_ln_g[0]), _row(conv_ln_b[0])
    p['conv_w_pw'] = conv_w_pw[0].astype(_BF16)
    p['ssm_w'], p['ssm_mv'], p['ssm_coef'] = _ssm_tables(
        ssm_lam_re[0], ssm_lam_im[0], ssm_log_dt[0], ssm_b_re[0], ssm_b_im[0],
        ssm_c_re[0], ssm_c_im[0])
    p['ssm_d'], p['ssm_w_glu'] = _row(ssm_d[0]), ssm_w_glu[0].astype(_BF16)
    p['w_out'] = w_out[0].astype(_BF16)
    p['xattn_wq'], p['xattn_wo'] = xattn_wq[0].astype(_BF16), xattn_wo[0].astype(_BF16)
    p['xattn_wkt'] = xattn_wkv[0][:, :D_MODEL].T.astype(_BF16)
    p['xattn_wv'] = xattn_wkv[0][:, D_MODEL:].astype(_BF16)
    return (_trunk(x_prompt, mem_prompt, p), _trunk(x_sample, mem_sample, p))
```

```python
import functools

import jax
import jax.numpy as jnp
import numpy as np
from jax import lax
from jax.experimental import pallas as pl
from jax.experimental.pallas import tpu as pltpu

D_MODEL = 1024
D_FF = 2816
FF_CHUNK = 256
N_FF_CHUNKS = D_FF // FF_CHUNK
CONV_WIDTH = 512
CONV_K = 31
CONV_PAD = CONV_K // 2
SUBLANES = 8
LANES = 128
HALO = 2 * SUBLANES
CONV_ROWS = 256
CONV_COLS = 128
SSM_WIDTH = 512
SSM_GROUP = 16
SSM_GROUPS = SSM_WIDTH // SSM_GROUP
SSM_STATE = 64
CHUNK = 16
FLAT = CHUNK * SSM_GROUP
GROUPS_PER_BLOCK = LANES // SSM_GROUP
RELAYOUT_ROWS_IN = 64
RELAYOUT_ROWS_OUT = 64
SCAN_STEPS = 8
POWER_ROWS = 24
N_MEM = 256
X_HEADS = 4
X_HEAD_DIM = D_MODEL // X_HEADS
EPS = 1e-6
IN_COLS = 2 * CONV_WIDTH + SSM_WIDTH + 2 * D_MODEL

ROW_TILE = 512
FFN_ROW_TILE = 512
FFN1_ROW_TILE = 1024
ATTN_ROW_TILE = 1024
VMEM_LIMIT = 60 * 1024 * 1024
SCAN_VMEM_BUDGET = 44 * 1024 * 1024

_F32 = jnp.float32
_BF16 = jnp.bfloat16


def _dot(a, b):
    return jnp.dot(a, b, preferred_element_type=_F32)


def _rmsnorm(x, g):
    return x * lax.rsqrt(jnp.mean(x * x, axis=-1, keepdims=True) + EPS) * g


def _const_spec(shape):
    nd = len(shape)
    return pl.BlockSpec(shape, lambda *_: (0,) * nd, pipeline_mode=pl.Buffered(1))


def _swiglu(xn, wgu_ref, wd_ref, act_ref):
    for j in range(N_FF_CHUNKS):
        lo = j * FF_CHUNK
        g = _dot(xn, wgu_ref[:, lo:lo + FF_CHUNK])
        u = _dot(xn, wgu_ref[:, D_FF + lo:D_FF + lo + FF_CHUNK])
        act_ref[:, lo:lo + FF_CHUNK] = (g * jax.nn.sigmoid(g) * u).astype(_BF16)
    return _dot(act_ref[...], wd_ref[...])


def _ffn1_kernel(x_ref, g1_ref, wgu_ref, wd_ref, gmix_ref, h1_ref, un_ref, act_ref):
    x = x_ref[...]
    xn = _rmsnorm(x, g1_ref[...]).astype(_BF16)
    h1 = x + 0.5 * _swiglu(xn, wgu_ref, wd_ref, act_ref)
    h1_ref[...] = h1
    un_ref[...] = _rmsnorm(h1, gmix_ref[...]).astype(_BF16)


def _inproj_kernel(un_ref, win_ref, bin_ref, v_ref, zs_ref, gate_ref):
    z = _dot(un_ref[...], win_ref[...]) + bin_ref[...]
    v_ref[...] = z[:, :CONV_WIDTH] * jax.nn.sigmoid(z[:, CONV_WIDTH:2 * CONV_WIDTH])
    zs_ref[...] = z[:, 2 * CONV_WIDTH:2 * CONV_WIDTH + SSM_WIDTH]
    gate_ref[...] = jax.nn.sigmoid(z[:, 2 * CONV_WIDTH + SSM_WIDTH:])


def _ffn_inproj(x, g1, wgu, wd, gmix, win, b_in):
    rows = x.shape[0]
    row = lambda tm, w: pl.BlockSpec((tm, w), lambda i: (i, 0))
    params = pltpu.CompilerParams(dimension_semantics=("arbitrary",), vmem_limit_bytes=VMEM_LIMIT)
    tm = FFN1_ROW_TILE
    h1, un = pl.pallas_call(
        _ffn1_kernel,
        grid=(rows // tm,),
        in_specs=[row(tm, D_MODEL), _const_spec((1, D_MODEL)), _const_spec(wgu.shape),
                  _const_spec(wd.shape), _const_spec((1, D_MODEL))],
        out_specs=[row(tm, D_MODEL), row(tm, D_MODEL)],
        out_shape=[jax.ShapeDtypeStruct((rows, D_MODEL), _F32),
                   jax.ShapeDtypeStruct((rows, D_MODEL), _BF16)],
        scratch_shapes=[pltpu.VMEM((tm, D_FF), _BF16)],
        compiler_params=params,
        name="ffn1",
    )(x, g1, wgu, wd, gmix)
    tm = FFN_ROW_TILE
    v, zs, gate = pl.pallas_call(
        _inproj_kernel,
        grid=(rows // tm,),
        in_specs=[row(tm, D_MODEL), _const_spec(win.shape), _const_spec((1, IN_COLS))],
        out_specs=[row(tm, CONV_WIDTH), row(tm, SSM_WIDTH), row(tm, 2 * D_MODEL)],
        out_shape=[jax.ShapeDtypeStruct((rows, CONV_WIDTH), _F32),
                   jax.ShapeDtypeStruct((rows, SSM_WIDTH), _F32),
                   jax.ShapeDtypeStruct((rows, 2 * D_MODEL), _F32)],
        compiler_params=params,
        name="inproj",
    )(un, win, b_in)
    return h1, v, zs, gate


def _piece_masks(shape):
    piece = lax.broadcasted_iota(jnp.int32, shape, 1) // SSM_GROUP
    return [piece == p for p in range(1, GROUPS_PER_BLOCK)]


def _diagonal_select(xs, masks):
    nb = GROUPS_PER_BLOCK
    out = []
    for k in range(nb):
        acc = xs[-k % nb]
        for p in range(1, nb):
            acc = jnp.where(masks[p - 1], xs[(p - k) % nb], acc)
        out.append(acc)
    return out


def _ssm_scan_kernel(z_ref, w_ref, mv_ref, coef_ref, y_ref, a_ref, sf_ref, sb_ref, yy_ref,
                     *, n_chunks):
    nb = GROUPS_PER_BLOCK
    halves = CHUNK // nb

    def relayout_in(i, carry):
        rr = RELAYOUT_ROWS_IN
        row0 = pl.multiple_of(i * (rr * CHUNK), rr * CHUNK)
        crow = pl.multiple_of(i * rr, rr)
        masks = _piece_masks((rr // 2, LANES))
        for half in range(halves):
            xs = [pltpu.bitcast(z_ref[pl.ds(row0 + half * nb + t, rr, stride=CHUNK), :].astype(_BF16),
                                jnp.uint32) for t in range(nb)]
            xs = [x if t == 0 else pltpu.roll(x, SSM_GROUP * t, 1) for t, x in enumerate(xs)]
            for g, x in enumerate(_diagonal_select(xs, masks)):
                a_ref[g, pl.ds(crow, rr), half * LANES:(half + 1) * LANES] = pltpu.bitcast(x, _BF16)
        return carry

    lax.fori_loop(0, n_chunks // RELAYOUT_ROWS_IN, relayout_in, 0)

    for g in range(nb):
        s = _dot(a_ref[g], w_ref[g])
        packed = pl.ds(g, n_chunks, stride=nb)
        for d, s_ref in enumerate((sf_ref, sb_ref)):
            own = s[:, d * LANES:(d + 1) * LANES]
            s_ref[packed, :] = own
            yy_ref[d, packed, :] = pltpu.roll(own, SSM_STATE, 1)

    n_blocks = n_chunks // SCAN_STEPS
    span = SCAN_STEPS * nb
    s_refs = (sf_ref, sb_ref)
    coefs = [(coef_ref[3 * d], coef_ref[3 * d + 1], coef_ref[3 * d + 2]) for d in range(2)]

    def rows(blk):
        return pl.ds(pl.multiple_of(blk * span, span), span)

    def load(d, blk):
        return s_refs[d][rows(blk), :], yy_ref[d, rows(blk), :]

    def advance(d, blk, state):
        v, w, s_all, sw_all = state
        a, b, bp = coefs[d]
        entering = [None] * SCAN_STEPS
        for k in (range(SCAN_STEPS) if d == 0 else range(SCAN_STEPS - 1, -1, -1)):
            entering[k] = v
            own = slice(k * nb, (k + 1) * nb)
            v, w = a * v + b * w + s_all[own], a * w + bp * v + sw_all[own]
        s_refs[d][rows(blk), :] = jnp.concatenate(entering, axis=0)
        return v, w

    def scan_body(i, carry):
        fwd, bwd = carry
        blk_f, blk_b = i, n_blocks - 1 - i
        next_f = load(0, jnp.minimum(blk_f + 1, n_blocks - 1))
        next_b = load(1, jnp.maximum(blk_b - 1, 0))
        return advance(0, blk_f, fwd) + next_f, advance(1, blk_b, bwd) + next_b

    zero = jnp.zeros((nb, LANES), _F32)
    lax.fori_loop(0, n_blocks, scan_body,
                  ((zero, zero) + load(0, 0), (zero, zero) + load(1, n_blocks - 1)))

    for g in range(nb):
        lhs = jnp.concatenate([a_ref[g],
                               sf_ref[pl.ds(g, n_chunks, stride=nb), :].astype(_BF16),
                               sb_ref[pl.ds(g, n_chunks, stride=nb), :].astype(_BF16)], axis=1)
        yg = _dot(lhs, mv_ref[g])
        for half in range(halves):
            yy_ref[half, g * n_chunks:(g + 1) * n_chunks, :] = yg[:, half * LANES:(half + 1) * LANES]

    def relayout_out(i, carry):
        rr = RELAYOUT_ROWS_OUT
        row0 = pl.multiple_of(i * (rr * CHUNK), rr * CHUNK)
        crow = pl.multiple_of(i * rr, rr)
        masks = _piece_masks((rr, LANES))
        for half in range(halves):
            ys = [yy_ref[half, pl.ds(g * n_chunks + crow, rr), :] for g in range(nb)]
            for t, y in enumerate(_diagonal_select(ys, masks)):
                if t:
                    y = pltpu.roll(y, LANES - SSM_GROUP * t, 1)
                y_ref[pl.ds(row0 + half * nb + t, rr, stride=CHUNK), :] = y
        return carry

    lax.fori_loop(0, n_chunks // RELAYOUT_ROWS_OUT, relayout_out, 0)


def _ssm_scan(zs, w, mv, coef):
    bsz, length, _ = zs.shape
    n_chunks = length // CHUNK
    nb = GROUPS_PER_BLOCK
    block_bytes = length * LANES * 4
    scratch_bytes = n_chunks * nb * (FLAT * 2 + 2 * LANES * 4 + FLAT * 4)
    table_bytes = nb * (FLAT * FLAT + 2 * FLAT * FLAT) * 2
    double = 4 * block_bytes + scratch_bytes + table_bytes <= SCAN_VMEM_BUDGET
    seq_mode = pl.Buffered(2 if double else 1)
    once = pl.Buffered(1)
    seq_spec = pl.BlockSpec((None, length, LANES), lambda j, b: (b, 0, j), pipeline_mode=seq_mode)
    return pl.pallas_call(
        functools.partial(_ssm_scan_kernel, n_chunks=n_chunks),
        grid=(SSM_GROUPS // nb, bsz),
        in_specs=[seq_spec,
                  pl.BlockSpec((nb, FLAT, FLAT), lambda j, b: (j, 0, 0), pipeline_mode=once),
                  pl.BlockSpec((nb, 2 * FLAT, FLAT), lambda j, b: (j, 0, 0), pipeline_mode=once),
                  pl.BlockSpec((None, 6, nb, LANES), lambda j, b: (j, 0, 0, 0), pipeline_mode=once)],
        out_specs=seq_spec,
        out_shape=jax.ShapeDtypeStruct(zs.shape, _F32),
        scratch_shapes=[pltpu.VMEM((nb, n_chunks, FLAT), _BF16),
                        pltpu.VMEM((n_chunks * nb, LANES), _F32),
                        pltpu.VMEM((n_chunks * nb, LANES), _F32),
                        pltpu.VMEM((2, n_chunks * nb, LANES), _F32)],
        compiler_params=pltpu.CompilerParams(dimension_semantics=("arbitrary", "arbitrary"),
                                             vmem_limit_bytes=VMEM_LIMIT),
        name="ssm_scan",
    )(zs, w, mv, coef)


def _exact_dot(a, b):
    return jnp.dot(a, b, preferred_element_type=_F32, precision=lax.Precision.HIGHEST)


def _lane_window(cols, start):
    first, r = divmod(start, LANES)
    if r == 0:
        return cols[first]
    lane = lax.broadcasted_iota(jnp.int32, cols[first].shape, 1)
    return jnp.where(lane < LANES - r, pltpu.roll(cols[first], LANES - r, 1),
                     pltpu.roll(cols[first + 1], LANES - r, 1))


def _ssm_tables_kernel(pa_ref, pb_ref, ba_ref, bb_ref, ct_ref, pt_ref, bt_ref,
                       tile_ref, sel_ref, w_ref, mv_ref):
    nb = GROUPS_PER_BLOCK
    shift = (pl.program_id(0) % nb) * SSM_GROUP

    def rotated_block(s):
        return (s // nb) * nb + (s % nb + pl.program_id(0) % nb) % nb

    def rotate_pieces(x):
        return jnp.concatenate([pltpu.roll(x[:, :LANES], shift, 1),
                                pltpu.roll(x[:, LANES:], shift, 1)], axis=1)

    def rows_of(s):
        return pl.ds(pl.multiple_of(rotated_block(s) * SSM_GROUP, SSM_GROUP), SSM_GROUP)

    lag_stacks = []
    for d in range(2):
        for s in range(CHUNK):
            k = CHUNK - 1 - s if d == 0 else s
            blk = pa_ref[d, k:k + 1, :] * ba_ref[d] + pb_ref[d, k:k + 1, :] * bb_ref[d]
            w_ref[rows_of(s), d * LANES:(d + 1) * LANES] = blk.astype(_BF16)

        c_re = _exact_dot(ct_ref[d, 0], tile_ref[...])
        c_im = _exact_dot(ct_ref[d, 1], tile_ref[...])

        def c_times_power(which):
            p_re = _exact_dot(pt_ref[d, 0], sel_ref[which])
            p_im = _exact_dot(pt_ref[d, 1], sel_ref[which])
            return c_re * p_re - c_im * p_im, c_re * p_im + c_im * p_re

        v_re, v_im = c_times_power(d)
        base = FLAT + d * LANES
        mv_ref[base:base + SSM_STATE, :] = rotate_pieces(v_re).astype(_BF16)
        mv_ref[base + SSM_STATE:base + LANES, :] = rotate_pieces(-v_im).astype(_BF16)
        k_re, k_im = c_times_power(2 + d)
        lag_stacks.append(_exact_dot(bt_ref[d, 0], k_re) - _exact_dot(bt_ref[d, 1], k_im))

    zero = jnp.zeros((SSM_GROUP, LANES), _F32)
    fwd = [zero, zero, lag_stacks[0][:, :LANES], lag_stacks[0][:, LANES:], zero]
    bwd = [lag_stacks[1][:, :LANES], lag_stacks[1][:, LANES:], zero, zero, zero]
    for s in range(CHUNK):
        f0, b0 = FLAT - SSM_GROUP * s, SSM_GROUP * (CHUNK - 1 - s)
        blk = jnp.concatenate([_lane_window(fwd, f0) + _lane_window(bwd, b0),
                               _lane_window(fwd, f0 + LANES) + _lane_window(bwd, b0 + LANES)], axis=1)
        mv_ref[rows_of(s), :] = rotate_pieces(blk).astype(_BF16)


def _table_constants():
    t = np.arange(FLAT) // SSM_GROUP
    o = np.arange(FLAT) % SSM_GROUP
    tile = (np.arange(SSM_GROUP)[:, None] == o[None, :])
    k = np.arange(LANES)[:, None]
    sel = np.stack([k == t + 1, k == CHUNK - t, k == t, k == CHUNK - 1 - t])
    return jnp.asarray(tile, _F32), jnp.asarray(sel, _F32)


def _ssm_tables(lam_re, lam_im, log_dt, b_re, b_im, c_re, c_im):
    G, N = SSM_GROUPS, SSM_STATE
    dt = jnp.exp(log_dt)[..., None]
    mag = jnp.exp(lam_re * dt)
    lr, li = mag * jnp.cos(lam_im * dt), mag * jnp.sin(lam_im * dt)
    p_re, p_im = lr[:, :, None], li[:, :, None]
    while p_re.shape[2] < CHUNK:
        top_re, top_im = p_re[:, :, -1:], p_im[:, :, -1:]
        p_re, p_im = (jnp.concatenate([p_re, p_re * top_re - p_im * top_im], axis=2),
                      jnp.concatenate([p_im, p_re * top_im + p_im * top_re], axis=2))
    p_re = jnp.concatenate([jnp.ones_like(lr)[:, :, None], p_re], axis=2)
    p_im = jnp.concatenate([jnp.zeros_like(li)[:, :, None], p_im], axis=2)
    den = lam_re * lam_re + lam_im * lam_im
    qr = ((lr - 1.0) * lam_re + li * lam_im) / den
    qi = (li * lam_re - (lr - 1.0) * lam_im) / den
    bt_re = (qr[..., None] * b_re - qi[..., None] * b_im).transpose(0, 1, 3, 2)
    bt_im = (qr[..., None] * b_im + qi[..., None] * b_re).transpose(0, 1, 3, 2)
    pad_k = lambda x: jnp.pad(x, ((0, 0), (0, 0), (0, POWER_ROWS - CHUNK - 1), (0, 0)))
    per_group = lambda x: jnp.moveaxis(x, 1, 0)
    pa = per_group(pad_k(jnp.concatenate([p_re, p_im], -1)))
    pb = per_group(pad_k(jnp.concatenate([-p_im, p_re], -1)))
    ba = per_group(jnp.concatenate([bt_re, bt_re], -1))
    bb = per_group(jnp.concatenate([bt_im, bt_im], -1))
    ct = per_group(jnp.stack([c_re, c_im], axis=2).transpose(0, 1, 2, 4, 3))
    pt = jnp.stack([p_re, p_im], axis=2).transpose(0, 1, 2, 4, 3)
    pt = per_group(jnp.pad(pt, ((0, 0),) * 4 + ((0, LANES - CHUNK - 1),)))
    bt = per_group(jnp.stack([bt_re, bt_im], axis=2))
    tile, sel = _table_constants()
    group = lambda *tail: pl.BlockSpec((None,) + tail, lambda g: (g,) + (0,) * len(tail))
    const = lambda a: pl.BlockSpec(a.shape, lambda g: (0,) * a.ndim, pipeline_mode=pl.Buffered(1))
    w, mv = pl.pallas_call(
        _ssm_tables_kernel,
        grid=(G,),
        in_specs=[group(2, POWER_ROWS, LANES), group(2, POWER_ROWS, LANES),
                  group(2, SSM_GROUP, LANES), group(2, SSM_GROUP, LANES),
                  group(2, 2, N, SSM_GROUP), group(2, 2, N, LANES), group(2, 2, SSM_GROUP, N),
                  const(tile), const(sel)],
        out_specs=[group(FLAT, FLAT), group(2 * FLAT, FLAT)],
        out_shape=[jax.ShapeDtypeStruct((G, FLAT, FLAT), _BF16),
                   jax.ShapeDtypeStruct((G, 2 * FLAT, FLAT), _BF16)],
        compiler_params=pltpu.CompilerParams(dimension_semantics=("arbitrary",)),
        name="ssm_tables",
    )(pa, pb, ba, bb, ct, pt, bt, tile, sel)
    fr, fi = p_re[:, :, CHUNK], p_im[:, :, CHUNK]
    coef = jnp.stack([jnp.concatenate([fr, fr], -1), jnp.concatenate([-fi, fi], -1),
                      jnp.concatenate([fi, -fi], -1)], axis=1)
    coef = coef.reshape(6, G // GROUPS_PER_BLOCK, GROUPS_PER_BLOCK, 2 * N).transpose(1, 0, 2, 3)
    return w, mv, coef


def _depthwise_conv(buf_ref, cw_ref, c_ref, tm):
    n_tiles = CONV_ROWS // SUBLANES
    sublane = lax.broadcasted_iota(jnp.int32, (SUBLANES, CONV_COLS), 0)

    def rows(i, carry):
        r0 = pl.multiple_of(i * CONV_ROWS, CONV_ROWS)
        for c0 in range(0, CONV_WIDTH, CONV_COLS):
            cols = slice(c0, c0 + CONV_COLS)
            out = [None] * n_tiles
            for r in range(SUBLANES):
                part = None
                for m in range(HALO * 2 // SUBLANES):
                    k = SUBLANES * m + r - (HALO - CONV_PAD)
                    if 0 <= k < CONV_K:
                        term = (buf_ref[pl.ds(r0 + SUBLANES * m, CONV_ROWS + SUBLANES), cols]
                                * cw_ref[k:k + 1, cols])
                        part = term if part is None else part + term
                tiles = [part[SUBLANES * j:SUBLANES * (j + 1)] for j in range(n_tiles + 1)]
                for j in range(n_tiles):
                    if r == 0:
                        shifted = tiles[j]
                    else:
                        shifted = pltpu.roll(jnp.where(sublane < r, tiles[j + 1], tiles[j]),
                                             SUBLANES - r, 0)
                    out[j] = shifted if out[j] is None else out[j] + shifted
            c_ref[pl.ds(r0, CONV_ROWS), cols] = jnp.concatenate(out, axis=0)
        return carry

    lax.fori_loop(0, tm // CONV_ROWS, rows, 0)


def _mix_kernel(v_ref, vp_ref, vn_ref, zs_ref, ys_ref, gate_ref, h1_ref,
                cw_ref, cb_ref, lng_ref, lnb_ref, wpw_ref, d_ref, wglu_ref, wout_ref,
                h2_ref, buf_ref, c_ref, *, tm):
    i = pl.program_id(1)
    last = pl.num_programs(1) - 1
    buf_ref[0:HALO, :] = jnp.where(i > 0, vp_ref[...], 0.0)
    buf_ref[HALO:HALO + tm, :] = v_ref[...]
    buf_ref[HALO + tm:2 * HALO + tm, :] = jnp.where(i < last, vn_ref[...], 0.0)
    _depthwise_conv(buf_ref, cw_ref, c_ref, tm)
    c = c_ref[...] + cb_ref[...]
    c = c - jnp.mean(c, axis=-1, keepdims=True)
    c = c * lax.rsqrt(jnp.mean(c * c, axis=-1, keepdims=True) + EPS) * lng_ref[...] + lnb_ref[...]
    conv_out = _dot((c * jax.nn.sigmoid(c)).astype(_BF16), wpw_ref[...])

    y = ys_ref[...] + d_ref[...] * zs_ref[...]
    ag = _dot(jax.nn.gelu(y).astype(_BF16), wglu_ref[...])
    ssm_out = ag[:, :D_MODEL] * jax.nn.sigmoid(ag[:, D_MODEL:])

    gate = gate_ref[...]
    merged = gate[:, :D_MODEL] * conv_out + gate[:, D_MODEL:] * ssm_out
    h2_ref[...] = h1_ref[...] + _dot(merged.astype(_BF16), wout_ref[...])


def _mix(v, zs, ys, gate, h1, cw, cb, lng, lnb, wpw, d, wglu, wout):
    bsz, length, _ = v.shape
    tm = ROW_TILE
    per = tm // HALO
    n_halo = length // HALO
    seq = lambda w: pl.BlockSpec((None, tm, w), lambda b, i: (b, i, 0))
    prev = pl.BlockSpec((None, HALO, CONV_WIDTH), lambda b, i: (b, jnp.maximum(i * per - 1, 0), 0))
    nxt = pl.BlockSpec((None, HALO, CONV_WIDTH),
                       lambda b, i: (b, jnp.minimum((i + 1) * per, n_halo - 1), 0))
    return pl.pallas_call(
        functools.partial(_mix_kernel, tm=tm),
        grid=(bsz, length // tm),
        in_specs=[seq(CONV_WIDTH), prev, nxt, seq(SSM_WIDTH), seq(SSM_WIDTH), seq(2 * D_MODEL),
                  seq(D_MODEL),
                  _const_spec(cw.shape), _const_spec((1, CONV_WIDTH)), _const_spec((1, CONV_WIDTH)),
                  _const_spec((1, CONV_WIDTH)), _const_spec(wpw.shape), _const_spec((1, SSM_WIDTH)),
                  _const_spec(wglu.shape), _const_spec(wout.shape)],
        out_specs=seq(D_MODEL),
        out_shape=jax.ShapeDtypeStruct((bsz, length, D_MODEL), _F32),
        scratch_shapes=[pltpu.VMEM((tm + 2 * HALO, CONV_WIDTH), _F32),
                        pltpu.VMEM((tm, CONV_WIDTH), _F32)],
        compiler_params=pltpu.CompilerParams(dimension_semantics=("arbitrary", "arbitrary"),
                                             vmem_limit_bytes=VMEM_LIMIT),
        name="mix",
    )(v, v, v, zs, ys, gate, h1, cw, cb, lng, lnb, wpw, d, wglu, wout)


def _mem_kv_kernel(mem_ref, g_ref, wkt_ref, wv_ref, kt_ref, v_ref):
    m = _rmsnorm(mem_ref[...], g_ref[...]).astype(_BF16)
    kt = lax.dot_general(wkt_ref[...], m, (((1,), (1,)), ((), ())), preferred_element_type=_F32)
    kt_ref[...] = kt.astype(_BF16)
    v_ref[...] = _dot(m, wv_ref[...]).astype(_BF16)


def _mem_kv(mem, g, wkt, wv):
    bsz = mem.shape[0]
    return pl.pallas_call(
        _mem_kv_kernel,
        grid=(bsz,),
        in_specs=[pl.BlockSpec((None, N_MEM, D_MODEL), lambda b: (b, 0, 0)),
                  _const_spec((1, D_MODEL)), _const_spec(wkt.shape), _const_spec(wv.shape)],
        out_specs=[pl.BlockSpec((None, D_MODEL, N_MEM), lambda b: (b, 0, 0)),
                   pl.BlockSpec((None, N_MEM, D_MODEL), lambda b: (b, 0, 0))],
        out_shape=[jax.ShapeDtypeStruct((bsz, D_MODEL, N_MEM), _BF16),
                   jax.ShapeDtypeStruct((bsz, N_MEM, D_MODEL), _BF16)],
        compiler_params=pltpu.CompilerParams(dimension_semantics=("arbitrary",)),
        name="mem_kv",
    )(mem, g, wkt, wv)


def _attn_ffn_kernel(h_ref, kt_ref, v_ref, gx_ref, wq_ref, wo_ref, g2_ref, wgu_ref, wd_ref,
                     gf_ref, out_ref, act_ref, o_ref):
    h = h_ref[...]
    q = _dot(_rmsnorm(h, gx_ref[...]).astype(_BF16), wq_ref[...]).astype(_BF16)
    for hd in range(X_HEADS):
        sl = slice(hd * X_HEAD_DIM, (hd + 1) * X_HEAD_DIM)
        s = _dot(q[:, sl], kt_ref[sl, :]) * (X_HEAD_DIM ** -0.5)
        e = jnp.exp(s - jnp.max(s, axis=-1, keepdims=True))
        p = e / jnp.sum(e, axis=-1, keepdims=True)
        o_ref[:, sl] = _dot(p.astype(_BF16), v_ref[:, sl]).astype(_BF16)
    h = h + _dot(o_ref[...], wo_ref[...])
    xn = _rmsnorm(h, g2_ref[...]).astype(_BF16)
    h = h + 0.5 * _swiglu(xn, wgu_ref, wd_ref, act_ref)
    out_ref[...] = _rmsnorm(h, gf_ref[...])


def _attn_ffn(h, kt, v, gx, wq, wo, g2, wgu, wd, gf):
    bsz, length, _ = h.shape
    tm = ATTN_ROW_TILE
    seq = pl.BlockSpec((None, tm, D_MODEL), lambda b, i: (b, i, 0))
    return pl.pallas_call(
        _attn_ffn_kernel,
        grid=(bsz, length // tm),
        in_specs=[seq,
                  pl.BlockSpec((None, D_MODEL, N_MEM), lambda b, i: (b, 0, 0)),
                  pl.BlockSpec((None, N_MEM, D_MODEL), lambda b, i: (b, 0, 0)),
                  _const_spec((1, D_MODEL)), _const_spec(wq.shape), _const_spec(wo.shape),
                  _const_spec((1, D_MODEL)), _const_spec(wgu.shape), _const_spec(wd.shape),
                  _const_spec((1, D_MODEL))],
        out_specs=seq,
        out_shape=jax.ShapeDtypeStruct(h.shape, _F32),
        scratch_shapes=[pltpu.VMEM((tm, D_FF), _BF16), pltpu.VMEM((tm, D_MODEL), _BF16)],
        compiler_params=pltpu.CompilerParams(dimension_semantics=("arbitrary", "arbitrary"),
                                             vmem_limit_bytes=VMEM_LIMIT),
        name="attn_ffn",
    )(h, kt, v, gx, wq, wo, g2, wgu, wd, gf)


def _row(vec):
    return vec.reshape(1, -1).astype(_F32)


def _trunk(x, mem, p):
    bsz, length, _ = x.shape
    h1, v, zs, gate = _ffn_inproj(x.reshape(bsz * length, D_MODEL), p['ffn1_g'], p['ffn1_wgu'],
                                  p['ffn1_wd'], p['mix_g'], p['w_in'], p['b_in'])
    seq = lambda a: a.reshape(bsz, length, a.shape[-1])
    h1, v, zs, gate = seq(h1), seq(v), seq(zs), seq(gate)
    ys = _ssm_scan(zs, p['ssm_w'], p['ssm_mv'], p['ssm_coef'])
    h2 = _mix(v, zs, ys, gate, h1, p['conv_w'], p['conv_b'], p['conv_ln_g'], p['conv_ln_b'],
              p['conv_w_pw'], p['ssm_d'], p['ssm_w_glu'], p['w_out'])
    kt, vm = _mem_kv(mem, p['mem_g'], p['xattn_wkt'], p['xattn_wv'])
    return _attn_ffn(h2, kt, vm, p['xattn_g'], p['xattn_wq'], p['xattn_wo'], p['ffn2_g'],
                     p['ffn2_wgu'], p['ffn2_wd'], p['final_g'])


def kernel(x_prompt, x_sample, mem_prompt, mem_sample, ffn1_g, ffn1_wgu, ffn1_wd, mix_g, w_in, b_in, conv_w, conv_b, conv_ln_g, conv_ln_b, conv_w_pw, ssm_lam_re, ssm_lam_im, ssm_log_dt, ssm_b_re, ssm_b_im, ssm_c_re, ssm_c_im, ssm_d, ssm_w_glu, w_out, xattn_g, mem_g, xattn_wq, xattn_wkv, xattn_wo, ffn2_g, ffn2_wgu, ffn2_wd, final_g):
    assert ffn1_g.shape[0] == 1, "single-layer trunk"
    p = {}
    p['ffn1_g'], p['mix_g'], p['xattn_g'] = _row(ffn1_g[0]), _row(mix_g[0]), _row(xattn_g[0])
    p['mem_g'], p['ffn2_g'], p['final_g'] = _row(mem_g[0]), _row(ffn2_g[0]), _row(final_g)
    p['ffn1_wgu'], p['ffn2_wgu'] = ffn1_wgu[0].astype(_BF16), ffn2_wgu[0].astype(_BF16)
    p['ffn1_wd'], p['ffn2_wd'] = ffn1_wd[0].astype(_BF16), ffn2_wd[0].astype(_BF16)
    p['w_in'], p['b_in'] = w_in[0].astype(_BF16), _row(b_in[0])
    p['conv_w'], p['conv_b'] = conv_w[0].astype(_F32), _row(conv_b[0])
    p['conv_ln_g'], p['conv_ln_b'] = _row(conv_ln_g[0]), _row(conv_ln_b[0])
    p['conv_w_pw'] = conv_w_pw[0].astype(_BF16)
    p['ssm_w'], p['ssm_mv'], p['ssm_coef'] = _ssm_tables(
        ssm_lam_re[0], ssm_lam_im[0], ssm_log_dt[0], ssm_b_re[0], ssm_b_im[0],
        ssm_c_re[0], ssm_c_im[0])
    p['ssm_d'], p['ssm_w_glu'] = _row(ssm_d[0]), ssm_w_glu[0].astype(_BF16)
    p['w_out'] = w_out[0].astype(_BF16)
    p['xattn_wq'], p['xattn_wo'] = xattn_wq[0].astype(_BF16), xattn_wo[0].astype(_BF16)
    p['xattn_wkt'] = xattn_wkv[0][:, :D_MODEL].T.astype(_BF16)
    p['xattn_wv'] = xattn_wkv[0][:, D_MODEL:].astype(_BF16)
    return (_trunk(x_prompt, mem_prompt, p), _trunk(x_sample, mem_sample, p))
```
